```python
import jax
import jax.numpy as jnp
from jax import lax
import numpy as np

D_MODEL = 2048
BATCH = 16
SEQ = 2048
DEPTH = 2
DEC_BATCH = 1
DEC_SEQ = 16384
PAST_LEN = 128

N_META = 16
HEAD_DIM = 64
ATTN_WIDTH = D_MODEL // 2
ATTN_HEADS = ATTN_WIDTH // HEAD_DIM
ATTN_KV_HEADS = ATTN_HEADS // 4
KV_WIDTH = ATTN_KV_HEADS * HEAD_DIM
WINDOW = 128
BLOCK = 128
HGRN_DK = 128
HGRN_DV = 128
HGRN_WIDTH = D_MODEL - ATTN_WIDTH
HGRN_HEADS = HGRN_WIDTH // HGRN_DV
HGRN_KEY_WIDTH = HGRN_HEADS * HGRN_DK
CHUNK = 64
MIX_WIDTH = ATTN_WIDTH + HGRN_WIDTH
D_FF = 11 * D_MODEL // 4
CONV_WIDTH = 3
EPS = 1e-6
NEG_INF = -1e30
IN_SPLITS = (ATTN_WIDTH, KV_WIDTH, KV_WIDTH, HGRN_KEY_WIDTH, HGRN_KEY_WIDTH, HGRN_KEY_WIDTH, HGRN_WIDTH, HGRN_WIDTH)
IN_COLS = sum(IN_SPLITS)

kernel_name = 'hymba_hgrn2_bidir_encoder'


def rmsnorm(x, gain):
    xf = x.astype(jnp.float32)
    y = xf * lax.rsqrt(jnp.mean(xf * xf, axis=-1, keepdims=True) + EPS)
    return (y * gain.astype(jnp.float32)).astype(x.dtype)


def alibi_slopes():
    return jnp.asarray(2.0 ** (-8.0 * np.arange(1, ATTN_HEADS + 1) / ATTN_HEADS), dtype=jnp.float32)


def sink_softmax(logits, sink):
    m = jnp.maximum(jnp.max(logits, axis=-1, keepdims=True), sink)
    p = jnp.exp(logits - m)
    return p / (jnp.sum(p, axis=-1, keepdims=True) + jnp.exp(sink - m))


def windowed_gqa(q, k, v, sink):
    f32 = jnp.float32
    B, L = q.shape[0], q.shape[1]
    S = L - N_META
    NB = S // BLOCK
    G = ATTN_HEADS // ATTN_KV_HEADS
    slopes = alibi_slopes().reshape(ATTN_KV_HEADS, G)[:, :, None, None]
    sink_b = sink.astype(f32).reshape(ATTN_KV_HEADS, G)[:, :, None, None]
    q = (q * HEAD_DIM ** -0.5).reshape(B, L, ATTN_KV_HEADS, G, HEAD_DIM)
    q_m, q_r = q[:, :N_META], q[:, N_META:]
    k_m, k_r = k[:, :N_META], k[:, N_META:]
    v_m, v_r = v[:, :N_META], v[:, N_META:]
    s_mm = jnp.einsum('btkgd,bmkd->bkgtm', q_m, k_m, preferred_element_type=f32)
    s_mr = jnp.einsum('btkgd,brkd->bkgtr', q_m, k_r[:, :BLOCK], preferred_element_type=f32)
    dist_mr = (N_META + jnp.arange(BLOCK)[None, :] - jnp.arange(N_META)[:, None]).astype(f32)
    s_mr = jnp.where(dist_mr <= WINDOW, s_mr - slopes * dist_mr, NEG_INF)
    p = sink_softmax(jnp.concatenate([s_mm, s_mr], axis=-1), sink_b).astype(v.dtype)
    o_m = (jnp.einsum('bkgtm,bmkd->btkgd', p[..., :N_META], v_m)
           + jnp.einsum('bkgtr,brkd->btkgd', p[..., N_META:], v_r[:, :BLOCK]))
    q_b = q_r.reshape(B, NB, BLOCK, ATTN_KV_HEADS, G, HEAD_DIM)

    def band(t):
        pad = jnp.zeros((B, BLOCK, ATTN_KV_HEADS, HEAD_DIM), t.dtype)
        tp = jnp.concatenate([pad, t, pad], axis=1).reshape(B, NB + 2, BLOCK, ATTN_KV_HEADS, HEAD_DIM)
        return jnp.concatenate([tp[:, :-2], tp[:, 1:-1], tp[:, 2:]], axis=2)

    k_w, v_w = band(k_r), band(v_r)
    s_w = jnp.einsum('bnikgd,bnjkd->bnkgij', q_b, k_w, preferred_element_type=f32)
    rel = jnp.arange(BLOCK)[:, None] - jnp.arange(3 * BLOCK)[None, :] + BLOCK
    key_pos = jnp.arange(NB)[:, None] * BLOCK + jnp.arange(3 * BLOCK)[None, :] - BLOCK
    valid = (jnp.abs(rel)[None] <= WINDOW) & ((key_pos >= 0) & (key_pos < S))[:, None, :]
    dist = jnp.abs(rel).astype(f32)
    s_w = jnp.where(valid[None, :, None, None], s_w - slopes * dist, NEG_INF)
    s_rm = jnp.einsum('bnikgd,bmkd->bnkgim', q_b, k_m, preferred_element_type=f32)
    p = sink_softmax(jnp.concatenate([s_rm, s_w], axis=-1), sink_b).astype(v.dtype)
    o_r = (jnp.einsum('bnkgim,bmkd->bnikgd', p[..., :N_META], v_m)
           + jnp.einsum('bnkgij,bnjkd->bnikgd', p[..., N_META:], v_w))
    return jnp.concatenate([o_m.reshape(B, N_META, ATTN_WIDTH), o_r.reshape(B, S, ATTN_WIDTH)], axis=1)


def hgrn_chunk(state, chunk):
    q, k, v, lf = chunk
    C = q.shape[1]
    a = jnp.cumsum(lf, axis=1)
    causal = jnp.tril(jnp.ones((C, C), dtype=bool))[None, :, :, None, None]
    decay = jnp.exp(jnp.where(causal, a[:, :, None] - a[:, None, :], -jnp.inf))
    scores = jnp.einsum('bthd,btshd,bshd->bhts', q, decay, k)
    o = (jnp.einsum('bhts,bshe->bthe', scores, v)
         + jnp.einsum('bthd,bhde->bthe', q * jnp.exp(a), state))
    a_last = a[:, -1]
    state = (jnp.exp(a_last)[..., None] * state
             + jnp.einsum('bshd,bshe->bhde', k * jnp.exp(a_last[:, None] - a), v))
    return state, o


def chunk_sweep(state, q, k, v, lf):
    B, S = q.shape[0], q.shape[1]
    NC = S // CHUNK

    def to_chunks(t):
        return jnp.moveaxis(t.reshape(B, NC, CHUNK, t.shape[2], t.shape[3]), 1, 0)

    state, o = lax.scan(hgrn_chunk, state, (to_chunks(q), to_chunks(k), to_chunks(v), to_chunks(lf)))
    return state, jnp.moveaxis(o, 0, 1).reshape(B, S, HGRN_HEADS, HGRN_DV)


def hgrn_gates(z, lb):
    lf = jnp.logaddexp(jnp.log(lb), jnp.log1p(-lb) + jax.nn.log_sigmoid(z))
    k = (1.0 - lb) * jax.nn.sigmoid(-z)
    return k, lf


def hgrn2_bidirectional(q, v, z_fwd, z_bwd, lb_fwd, lb_bwd):
    B = q.shape[0]
    s0 = jnp.zeros((B, HGRN_HEADS, HGRN_DK, HGRN_DV), jnp.float32)
    k_f, lf_f = hgrn_gates(z_fwd, lb_fwd)
    k_b, lf_b = hgrn_gates(z_bwd, lb_bwd)
    meta = lambda t: t[:, :N_META]
    real = lambda t: t[:, N_META:]
    rflip = lambda t: jnp.flip(t, axis=1)
    s1, o_fm = hgrn_chunk(s0, (meta(q), meta(k_f), meta(v), meta(lf_f)))
    _, o_fr = chunk_sweep(s1, real(q), real(k_f), real(v), real(lf_f))
    s2, o_br = chunk_sweep(s0, rflip(real(q)), rflip(real(k_b)), rflip(real(v)), rflip(real(lf_b)))
    _, o_bm = hgrn_chunk(s2, (rflip(meta(q)), rflip(meta(k_b)), rflip(meta(v)), rflip(meta(lf_b))))
    return jnp.concatenate([o_fm + rflip(o_bm), o_fr + rflip(o_br)], axis=1)


def conv_ffn(h, w_up, conv_w, conv_b, w_down):
    u = h @ w_up
    u = lax.conv_general_dilated(u, conv_w[:, None, :].astype(u.dtype), window_strides=(1,),
                                 padding=((CONV_WIDTH // 2, CONV_WIDTH // 2),),
                                 dimension_numbers=('NWC', 'WIO', 'NWC'),
                                 feature_group_count=2 * D_FF) + conv_b
    a, b = jnp.split(u, 2, axis=-1)
    return (jax.nn.silu(a) * b) @ w_down


def forget_lower_bounds(lb_param):
    c = jnp.cumsum(jax.nn.softmax(lb_param.astype(jnp.float32), axis=1), axis=1)
    return c - c[:, :1]


def trunk(x, meta_tokens, mix_norm, w_in, attn_sink, attn_out_norm, lb, hgrn_out_norm, w_out,
          ffn_norm, w_up, conv_w, conv_b, w_down, final_norm):
    B, S = x.shape[0], x.shape[1]
    L = N_META + S
    meta = jnp.broadcast_to(meta_tokens.astype(x.dtype)[None], (B, N_META, D_MODEL))
    h = jnp.concatenate([meta, x], axis=1)
    cuts = np.cumsum(IN_SPLITS)[:-1].tolist()
    f32 = jnp.float32
    for l in range(DEPTH):
        u = rmsnorm(h, mix_norm[l])
        q_a, k_a, v_a, q_h, z_f, z_b, i_h, g_h = jnp.split(u @ w_in[l], cuts, axis=-1)
        attn = windowed_gqa(q_a.reshape(B, L, ATTN_HEADS, HEAD_DIM),
                            k_a.reshape(B, L, ATTN_KV_HEADS, HEAD_DIM),
                            v_a.reshape(B, L, ATTN_KV_HEADS, HEAD_DIM), attn_sink[l])
        attn = rmsnorm(attn, attn_out_norm[l])
        o = hgrn2_bidirectional(q_h.reshape(B, L, HGRN_HEADS, HGRN_DK).astype(f32),
                                i_h.reshape(B, L, HGRN_HEADS, HGRN_DV).astype(f32),
                                z_f.reshape(B, L, HGRN_HEADS, HGRN_DK).astype(f32),
                                z_b.reshape(B, L, HGRN_HEADS, HGRN_DK).astype(f32),
                                lb[0, l].reshape(HGRN_HEADS, HGRN_DK), lb[1, l].reshape(HGRN_HEADS, HGRN_DK))
        o = rmsnorm(o, hgrn_out_norm[l].reshape(HGRN_HEADS, HGRN_DV)) * jax.nn.silu(
            g_h.astype(f32)).reshape(B, L, HGRN_HEADS, HGRN_DV)
        rec = o.reshape(B, L, HGRN_WIDTH).astype(h.dtype)
        h = h + jnp.concatenate([attn, rec], axis=-1) @ w_out[l]
        h = h + conv_ffn(rmsnorm(h, ffn_norm[l]), w_up[l], conv_w[l], conv_b[l], w_down[l])
    return rmsnorm(h, final_norm)[:, N_META:]


def setup_inputs(seed: int = 0) -> dict:
    key = jax.random.key(seed)
    ks = jax.random.split(key, 16)
    n = jax.random.normal
    f32 = jnp.float32
    return {
        'x_prompt': n(ks[0], (BATCH, SEQ, D_MODEL), f32),
        'x_sample': n(ks[1], (DEC_BATCH, DEC_SEQ, D_MODEL), f32),
        'meta_tokens': n(ks[2], (N_META, D_MODEL), f32),
        'mix_norm': 1.0 + 0.01 * n(ks[3], (DEPTH, D_MODEL), f32),
        'w_in': n(ks[4], (DEPTH, D_MODEL, IN_COLS), f32) * D_MODEL ** -0.5,
        'attn_sink': n(ks[5], (DEPTH, ATTN_HEADS), f32),
        'attn_out_norm': 1.0 + 0.01 * n(ks[6], (DEPTH, ATTN_WIDTH), f32),
        'hgrn_lower_bounds': 0.5 * n(ks[7], (2, DEPTH, HGRN_KEY_WIDTH), f32),
        'hgrn_out_norm': 1.0 + 0.01 * n(ks[8], (DEPTH, HGRN_WIDTH), f32),
        'w_out': n(ks[9], (DEPTH, MIX_WIDTH, D_MODEL), f32) * MIX_WIDTH ** -0.5,
        'ffn_norm': 1.0 + 0.01 * n(ks[10], (DEPTH, D_MODEL), f32),
        'w_up': n(ks[11], (DEPTH, D_MODEL, 2 * D_FF), f32) * D_MODEL ** -0.5,
        'conv_w': n(ks[12], (DEPTH, CONV_WIDTH, 2 * D_FF), f32) * CONV_WIDTH ** -0.5,
        'conv_b': 0.01 * n(ks[13], (DEPTH, 2 * D_FF), f32),
        'w_down': n(ks[14], (DEPTH, D_FF, D_MODEL), f32) * D_FF ** -0.5,
        'final_norm': 1.0 + 0.01 * n(ks[15], (D_MODEL,), f32),
    }


def reference(x_prompt, x_sample, meta_tokens, mix_norm, w_in, attn_sink, attn_out_norm, hgrn_lower_bounds,
              hgrn_out_norm, w_out, ffn_norm, w_up, conv_w, conv_b, w_down, final_norm):
    lb = forget_lower_bounds(hgrn_lower_bounds)
    y_prompt = trunk(x_prompt, meta_tokens, mix_norm, w_in, attn_sink, attn_out_norm, lb, hgrn_out_norm,
                     w_out, ffn_norm, w_up, conv_w, conv_b, w_down, final_norm)
    y_sample = trunk(x_sample, meta_tokens, mix_norm, w_in, attn_sink, attn_out_norm, lb, hgrn_out_norm,
                     w_out, ffn_norm, w_up, conv_w, conv_b, w_down, final_norm)
    return (y_prompt, y_sample)
```

```python
import functools
from typing import NamedTuple

import numpy as np
import jax
import jax.numpy as jnp
from jax import lax
from jax.experimental import pallas as pl
from jax.experimental.pallas import tpu as pltpu

F32 = jnp.float32
BF16 = jnp.bfloat16

D_MODEL = 2048
N_META = 16
HEAD_DIM = 64
ATTN_WIDTH = 1024
ATTN_HEADS = 16
ATTN_KV_HEADS = 4
KV_WIDTH = ATTN_KV_HEADS * HEAD_DIM
BLK = 128
PAD_ROWS = BLK - N_META
HG_HEADS = 8
HG_DIM = 128
HG_WIDTH = HG_HEADS * HG_DIM
D_FF = 5632
EPS = 1e-6
NEG_INF = -1e30
KK_WIDTH = ATTN_KV_HEADS * 128
QKV_COLS = ATTN_WIDTH + 2 * KK_WIDTH
IN_COLS2 = QKV_COLS + 5 * HG_WIDTH

TM = 1024
TM_RES = 512
TN_IN = 512
TF = 512
HALO = 8
VMEM_LIMIT = 56 * 1024 * 1024


def _dot(a, b):
    return jnp.dot(a, b, preferred_element_type=F32)


def _dot_nt(a, b):
    return lax.dot_general(a, b, (((1,), (1,)), ((), ())), preferred_element_type=F32)


class _Layout(NamedTuple):
    nb_s: int
    nb_p: int
    base_p: int
    n_blocks: int
    seqs: tuple


def _make_layout(bp, sp, bs, ss):
    nb_s, nb_p = ss // BLK + 1, sp // BLK + 1
    base_p = -(-(bs * nb_s) // nb_p) * nb_p
    used = base_p + bp * nb_p
    per_tile = TM // BLK
    n_blocks = -(-used // per_tile) * per_tile
    seqs = tuple((b * nb_s, nb_s) for b in range(bs)) + tuple((base_p + b * nb_p, nb_p) for b in range(bp))
    return _Layout(nb_s, nb_p, base_p, n_blocks, seqs)


def _block_tables(lay):
    n = lay.n_blocks
    thr = np.full((n,), BLK, np.int32)
    prev = np.arange(n, dtype=np.int32)
    nxt = np.arange(n, dtype=np.int32)
    meta = np.arange(n, dtype=np.int32)
    flags = np.zeros((n,), np.int32)
    for first, cnt in lay.seqs:
        for i in range(cnt):
            blk = first + i
            thr[blk] = PAD_ROWS if i == 0 else 0
            prev[blk] = max(blk - 1, first)
            nxt[blk] = min(blk + 1, first + cnt - 1)
            meta[blk] = first
            flags[blk] = (1 if i >= 2 else 0) | (2 if i >= 1 else 0) | (4 if i + 1 <= cnt - 1 else 0)
    return thr, prev, nxt, meta, flags


def _inproj_kernel(x_ref, g_ref, w_ref, oa_ref, oh_ref, xn_ref, *, n_attn_tiles):
    j = pl.program_id(1)

    @pl.when(j == 0)
    def _():
        x = x_ref[...]
        ms = jnp.mean(x * x, axis=-1, keepdims=True)
        xn_ref[...] = (x * lax.rsqrt(ms + EPS) * g_ref[...]).astype(BF16)

    y = _dot(xn_ref[...], w_ref[...])

    @pl.when(j < n_attn_tiles)
    def _():
        oa_ref[...] = y.astype(BF16)

    @pl.when(j >= n_attn_tiles)
    def _():
        oh_ref[...] = y


def _inproj(x, gain, w):
    rows = x.shape[0]
    na = QKV_COLS // TN_IN
    nj = IN_COLS2 // TN_IN
    return pl.pallas_call(
        functools.partial(_inproj_kernel, n_attn_tiles=na),
        grid=(rows // TM, nj),
        in_specs=[
            pl.BlockSpec((TM, D_MODEL), lambda i, j: (i, 0)),
            pl.BlockSpec((1, D_MODEL), lambda i, j: (0, 0)),
            pl.BlockSpec((D_MODEL, TN_IN), lambda i, j: (0, j)),
        ],
        out_specs=[
            pl.BlockSpec((TM, TN_IN), lambda i, j: (i, jnp.minimum(j, na - 1))),
            pl.BlockSpec((TM, TN_IN), lambda i, j: (i, jnp.maximum(j - na, 0))),
        ],
        out_shape=[
            jax.ShapeDtypeStruct((rows, QKV_COLS), BF16),
            jax.ShapeDtypeStruct((rows, 5 * HG_WIDTH), F32),
        ],
        scratch_shapes=[pltpu.VMEM((TM, D_MODEL), BF16)],
        compiler_params=pltpu.CompilerParams(
            dimension_semantics=("arbitrary", "arbitrary"), vmem_limit_bytes=VMEM_LIMIT),
        name="inproj",
    )(x, gain, w)


def _attn_bias_tables():
    slopes = (2.0 ** (-8.0 * np.arange(1, ATTN_HEADS + 1) / ATTN_HEADS)).astype(np.float32)
    i = np.arange(BLK)[:, None]
    j = np.arange(BLK)[None, :]
    d_prev = (BLK + i - j).astype(np.float32)
    d_cur = np.abs(i - j).astype(np.float32)
    d_next = (BLK + j - i).astype(np.float32)
    neg = np.float32(NEG_INF)

    def tab(dist, valid):
        t = -(slopes[:, None, None] * dist[None])
        return np.where(valid[None], t, neg).astype(np.float32)

    ta = np.concatenate([tab(d_prev, d_prev <= BLK), tab(d_cur, d_cur <= BLK)], axis=2)
    tb = np.concatenate([tab(d_next, d_next <= BLK), np.zeros((ATTN_HEADS, BLK, N_META), np.float32)], axis=2)
    return ta, tb


def _attn_kernel(prev_t, next_t, meta_t, flag_t, q_ref, kp_ref, kc_ref, kn_ref, km_ref,
                 ta_ref, tb_ref, sink_ref, gain_ref, o_ref, acc_ref):
    del prev_t, next_t, meta_t
    fl = flag_t[pl.program_id(0)]
    lane_a = lax.broadcasted_iota(jnp.int32, (BLK, 2 * BLK), 1)
    lane_b = lax.broadcasted_iota(jnp.int32, (BLK, BLK + N_META), 1)
    mask_a = jnp.where(lane_a < BLK, fl & 1, fl & 2) > 0
    mask_b = jnp.where(lane_b < BLK, fl & 4, 1) > 0
    n_a, n_b = 2 * BLK, BLK + N_META
    lo_a = lax.broadcasted_iota(jnp.int32, (n_a, BLK), 1) < HEAD_DIM
    lo_b = lax.broadcasted_iota(jnp.int32, (n_b, BLK), 1) < HEAD_DIM
    zero = jnp.zeros((), BF16)

    for kh in range(ATTN_KV_HEADS):
        kcols = slice(kh * BLK, (kh + 1) * BLK)
        vcols = slice(KK_WIDTH + kh * BLK, KK_WIDTH + (kh + 1) * BLK)
        k_a = jnp.concatenate([kp_ref[:, kcols], kc_ref[:, kcols]], axis=0)
        k_b = jnp.concatenate([kn_ref[:, kcols], km_ref[PAD_ROWS:, kcols]], axis=0)
        v_a = jnp.concatenate([kp_ref[:, vcols], kc_ref[:, vcols]], axis=0)
        v_b = jnp.concatenate([kn_ref[:, vcols], km_ref[PAD_ROWS:, vcols]], axis=0)
        halves = [
            (jnp.where(lo_a, k_a, zero), jnp.where(lo_b, k_b, zero),
             jnp.where(lo_a, v_a, zero), jnp.where(lo_b, v_b, zero)),
            (jnp.where(lo_a, zero, k_a), jnp.where(lo_b, zero, k_b),
             jnp.where(lo_a, zero, v_a), jnp.where(lo_b, zero, v_b)),
        ]
        for j in range(2):
            col = (2 * kh + j) * BLK
            q_pair = q_ref[:, col:col + BLK]
            o_pair = None
            for e in range(2):
                h = 4 * kh + 2 * j + e
                ka_h, kb_h, va_h, vb_h = halves[e]
                s_a = jnp.where(mask_a, _dot_nt(q_pair, ka_h) + ta_ref[h], NEG_INF)
                s_b = jnp.where(mask_b, _dot_nt(q_pair, kb_h) + tb_ref[h], NEG_INF)
                sink = sink_ref[h]
                m = jnp.maximum(jnp.maximum(jnp.max(s_a, axis=-1, keepdims=True),
                                            jnp.max(s_b, axis=-1, keepdims=True)), sink)
                p_a = jnp.exp(s_a - m)
                p_b = jnp.exp(s_b - m)
                den = (jnp.sum(p_a, axis=-1, keepdims=True) + jnp.sum(p_b, axis=-1, keepdims=True)
                       + jnp.exp(sink - m))
                o_h = (_dot(p_a.astype(BF16), va_h) + _dot(p_b.astype(BF16), vb_h)) / den
                o_pair = o_h if o_pair is None else o_pair + o_h
            acc_ref[:, col:col + BLK] = o_pair

    o = acc_ref[...]
    ms = jnp.mean(o * o, axis=-1, keepdims=True)
    o_ref[...] = (o * lax.rsqrt(ms + EPS) * gain_ref[...]).astype(BF16)


def _attention(qkv, tables, ta, tb, sink, gain):
    _, prev, nxt, meta, flags = tables
    n_blocks = qkv.shape[0] // BLK
    kv_spec = lambda f: pl.BlockSpec((BLK, 2 * KK_WIDTH), f)
    grid_spec = pltpu.PrefetchScalarGridSpec(
        num_scalar_prefetch=4,
        grid=(n_blocks,),
        in_specs=[
            pl.BlockSpec((BLK, ATTN_WIDTH), lambda i, p, n, m, f: (i, 0)),
            kv_spec(lambda i, p, n, m, f: (p[i], 1)),
            kv_spec(lambda i, p, n, m, f: (i, 1)),
            kv_spec(lambda i, p, n, m, f: (n[i], 1)),
            kv_spec(lambda i, p, n, m, f: (m[i], 1)),
            pl.BlockSpec((ATTN_HEADS, BLK, 2 * BLK), lambda i, p, n, m, f: (0, 0, 0)),
            pl.BlockSpec((ATTN_HEADS, BLK, BLK + N_META), lambda i, p, n, m, f: (0, 0, 0)),
            pl.BlockSpec(memory_space=pltpu.SMEM),
            pl.BlockSpec((1, ATTN_WIDTH), lambda i, p, n, m, f: (0, 0)),
        ],
        out_specs=pl.BlockSpec((BLK, ATTN_WIDTH), lambda i, p, n, m, f: (i, 0)),
        scratch_shapes=[pltpu.VMEM((BLK, ATTN_WIDTH), F32)],
    )
    return pl.pallas_call(
        _attn_kernel,
        grid_spec=grid_spec,
        out_shape=jax.ShapeDtypeStruct((qkv.shape[0], ATTN_WIDTH), BF16),
        compiler_params=pltpu.CompilerParams(
            dimension_semantics=("arbitrary",), vmem_limit_bytes=VMEM_LIMIT),
        name="attention",
    )(jnp.asarray(prev), jnp.asarray(nxt), jnp.asarray(meta), jnp.asarray(flags),
      qkv, qkv, qkv, qkv, qkv, ta, tb, sink, gain)


_LEVELS = (64, 32, 16, 8, 4, 2, 1)


def _hgrn_consts():
    t = np.arange(BLK)[:, None]
    s = np.arange(BLK)[None, :]
    lmat = (s <= t).astype(np.float32)
    umat = (s >= t).astype(np.float32)
    x = t ^ s
    lv = np.where(x == 0, -1, np.floor(np.log2(np.maximum(x, 1)))).astype(np.int32)
    return jnp.asarray(lmat, BF16), jnp.asarray(umat, BF16), jnp.asarray(lv)


def _level_reference(ac_ref, c, rev):
    off = c - 1 + (1 if rev else 0)
    if 2 * c >= 8:
        pieces = [jnp.broadcast_to(ac_ref[bs + off:bs + off + 1, :], (2 * c, HG_DIM))
                  for bs in range(0, BLK, 2 * c)]
        return jnp.concatenate(pieces, axis=0)
    assert c == 2
    sub = lax.broadcasted_iota(jnp.int32, (8, HG_DIM), 0)
    pieces = []
    for bs in range(0, BLK, 8):
        r1 = jnp.broadcast_to(ac_ref[bs + off:bs + off + 1, :], (8, HG_DIM))
        r2 = jnp.broadcast_to(ac_ref[bs + 4 + off:bs + 4 + off + 1, :], (8, HG_DIM))
        pieces.append(jnp.where(sub < 4, r1, r2))
    return jnp.concatenate(pieces, axis=0)


def _hgrn_chunk(q, z, v, first_row, lbp, cmat, lv, ac_ref, st_ref, rev):
    off = 3 if rev else 0
    log_lb, log_1m_lb, one_m_lb = lbp[off:off + 1, :], lbp[off + 1:off + 2, :], lbp[off + 2:off + 3, :]
    row = lax.broadcasted_iota(jnp.int32, (BLK, HG_DIM), 0)
    valid = row >= first_row

    log_sig = jnp.minimum(z, 0.0) - jnp.log1p(jnp.exp(-jnp.abs(z)))
    bv = log_1m_lb + log_sig
    lf = jnp.maximum(log_lb, bv) + jnp.log1p(jnp.exp(-jnp.abs(log_lb - bv)))
    k = one_m_lb * jnp.exp(log_sig - z)
    lf = jnp.where(valid, lf, 0.0)
    k = jnp.where(valid, k, 0.0)

    hi = lf.astype(BF16)
    lo = (lf - hi.astype(F32)).astype(BF16)
    ac2 = _dot(cmat, jnp.concatenate([hi, lo], axis=1))
    ac = ac2[:, :HG_DIM] + ac2[:, HG_DIM:]
    ac_ref[...] = ac

    p = jnp.where(lv == -1, jnp.sum(q * k, axis=-1, keepdims=True), 0.0)
    for c in _LEVELS:
        q_rows = ((row & c) == 0) if rev else ((row & c) != 0)
        if c == 1:
            w = jnp.where(q_rows, jnp.exp(lf), 1.0)
        else:
            d = ac - _level_reference(ac_ref, c, rev)
            w = jnp.exp(jnp.where(q_rows, d, -d))
        qt = jnp.where(q_rows, q * w, 0.0).astype(BF16)
        kt = jnp.where(q_rows, 0.0, k * w).astype(BF16)
        p = jnp.where(lv == int(np.log2(c)), _dot_nt(qt, kt), p)

    v16 = v.astype(BF16)
    st = st_ref[...]
    o = _dot(p.astype(BF16), v16) + _dot_nt((q * jnp.exp(ac)).astype(BF16), st.astype(BF16))

    edge = ac_ref[0:1, :] if rev else ac_ref[BLK - 1:BLK, :]
    ke = (k * jnp.exp(edge - ac)).astype(BF16)
    st_ref[...] = st * jnp.exp(edge) + _dot(v.T.astype(BF16), ke)
    return o


def _hgrn_kernel(q_ref, zf_ref, zb_ref, v_ref, g_ref, lbp_ref, gn_ref, lmat_ref, umat_ref, lv_ref,
                 alias_ref, o_ref, st_ref, obuf_ref, ac_ref, *, nrb, tb):
    del alias_ref
    ph = pl.program_id(2)
    rb = pl.program_id(3)
    nch = tb // BLK
    lbp = lbp_ref
    lv = lv_ref[...]

    @pl.when(rb == 0)
    def _():
        st_ref[...] = jnp.zeros_like(st_ref)

    def sweep(rev):
        z_ref = zb_ref if rev else zf_ref
        cmat = umat_ref[...] if rev else lmat_ref[...]
        rb_seq = (nrb - 1 - rb) if rev else rb

        def body(ci, carry):
            cj = (nch - 1 - ci) if rev else ci
            r0 = pl.multiple_of(cj * BLK, BLK)
            rows = pl.ds(r0, BLK)
            first_row = jnp.where(rb_seq * nch + cj > 0, 0, PAD_ROWS)
            o = _hgrn_chunk(q_ref[rows, :], z_ref[rows, :], v_ref[rows, :], first_row, lbp, cmat, lv,
                            ac_ref, st_ref, rev)
            seq_rows = pl.ds(pl.multiple_of(rb_seq * tb + r0, BLK), BLK)
            if rev:
                obuf_ref[seq_rows, :] = o
            else:
                o = o + obuf_ref[seq_rows, :]
                ms = jnp.mean(o * o, axis=-1, keepdims=True)
                y = o * lax.rsqrt(ms + EPS) * gn_ref[...]
                g = g_ref[rows, :]
                o_ref[rows, :] = (y * (g / (1.0 + jnp.exp(-g)))).astype(BF16)
            return carry

        lax.fori_loop(0, nch, body, 0)

    @pl.when(ph == 0)
    def _():
        sweep(True)

    @pl.when(ph == 1)
    def _():
        sweep(False)


def _hgrn_group(hg, rec, lbp, gn, consts, *, base_blk, batch, nb_seq):
    kmax = nb_seq if nb_seq * BLK <= 4096 else max(k for k in range(1, 5) if nb_seq % k == 0)
    tb = kmax * BLK
    nrb = nb_seq // kmax
    assert (base_blk * BLK) % tb == 0
    base = base_blk * BLK // tb
    lmat, umat, lv = consts

    def rows_in(b, h, ph, rb):
        return base + b * nrb + jnp.where(ph == 0, nrb - 1 - rb, rb)

    def rows_fwd_only(b, h, ph, rb):
        return base + b * nrb + jnp.where(ph == 0, 0, rb)

    def rows_bwd_only(b, h, ph, rb):
        return base + b * nrb + jnp.where(ph == 0, nrb - 1 - rb, 0)

    blk = lambda rows_fn, col0: pl.BlockSpec(
        (tb, HG_DIM), lambda b, h, ph, rb: (rows_fn(b, h, ph, rb), col0 + h))
    const = lambda shape: pl.BlockSpec(shape, lambda b, h, ph, rb: (0, 0))
    seq_rows = nb_seq * BLK
    return pl.pallas_call(
        functools.partial(_hgrn_kernel, nrb=nrb, tb=tb),
        grid=(batch, HG_HEADS, 2, nrb),
        in_specs=[
            blk(rows_in, 0),
            blk(rows_fwd_only, HG_HEADS),
            blk(rows_bwd_only, 2 * HG_HEADS),
            blk(rows_in, 3 * HG_HEADS),
            blk(rows_fwd_only, 4 * HG_HEADS),
            pl.BlockSpec((8, HG_DIM), lambda b, h, ph, rb: (0, h)),
            pl.BlockSpec((1, HG_DIM), lambda b, h, ph, rb: (0, h)),
            const((BLK, BLK)), const((BLK, BLK)), const((BLK, BLK)),
            pl.BlockSpec(memory_space=pl.ANY),
        ],
        out_specs=blk(rows_fwd_only, 0),
        out_shape=jax.ShapeDtypeStruct(rec.shape, rec.dtype),
        input_output_aliases={10: 0},
        scratch_shapes=[
            pltpu.VMEM((HG_DIM, HG_DIM), F32),
            pltpu.VMEM((seq_rows, HG_DIM), F32),
            pltpu.VMEM((BLK, HG_DIM), F32),
        ],
        compiler_params=pltpu.CompilerParams(
            dimension_semantics=("arbitrary",) * 4, vmem_limit_bytes=VMEM_LIMIT),
        name="hgrn",
    )(hg, hg, hg, hg, hg, lbp, gn, lmat, umat, lv, rec)


def _store_valid_rows(o_ref, y, thr_ref, tile, rows_per_tile):
    row = lax.broadcasted_iota(jnp.int32, (BLK, y.shape[1]), 0)
    for kb in range(rows_per_tile // BLK):
        t = thr_ref[tile * (rows_per_tile // BLK) + kb]
        o_ref[kb * BLK:(kb + 1) * BLK, :] = jnp.where(row >= t, y[kb * BLK:(kb + 1) * BLK, :], 0.0)


def _outproj_kernel(thr_ref, a_ref, r_ref, x_ref, wa_ref, wr_ref, o_ref):
    y = x_ref[...] + _dot(a_ref[...], wa_ref[...]) + _dot(r_ref[...], wr_ref[...])
    _store_valid_rows(o_ref, y, thr_ref, pl.program_id(0), TM_RES)


def _outproj(thr, attn, rec, x, wa, wr):
    rows = x.shape[0]
    grid_spec = pltpu.PrefetchScalarGridSpec(
        num_scalar_prefetch=1,
        grid=(rows // TM_RES,),
        in_specs=[
            pl.BlockSpec((TM_RES, ATTN_WIDTH), lambda i, t: (i, 0)),
            pl.BlockSpec((TM_RES, HG_WIDTH), lambda i, t: (i, 0)),
            pl.BlockSpec((TM_RES, D_MODEL), lambda i, t: (i, 0)),
            pl.BlockSpec((ATTN_WIDTH, D_MODEL), lambda i, t: (0, 0)),
            pl.BlockSpec((HG_WIDTH, D_MODEL), lambda i, t: (0, 0)),
        ],
        out_specs=pl.BlockSpec((TM_RES, D_MODEL), lambda i, t: (i, 0)),
    )
    return pl.pallas_call(
        _outproj_kernel,
        grid_spec=grid_spec,
        out_shape=jax.ShapeDtypeStruct((rows, D_MODEL), F32),
        compiler_params=pltpu.CompilerParams(
            dimension_semantics=("arbitrary",), vmem_limit_bytes=VMEM_LIMIT),
        name="outproj",
    )(thr, attn, rec, x, wa, wr)


def _ffn_kernel(thr_ref, x_ref, xp_ref, xn_ref, g_ref, wa_ref, wb_ref, cwa_ref, cwb_ref, cba_ref, cbb_ref,
                wd_ref, gf_ref, o_ref, xs_ref, acc_ref, *, final):
    i = pl.program_id(0)
    f = pl.program_id(1)
    tm = x_ref.shape[0]
    ext = tm + 2 * HALO

    def norm(x):
        ms = jnp.mean(x * x, axis=-1, keepdims=True)
        return (x * lax.rsqrt(ms + EPS) * g_ref[...]).astype(BF16)

    @pl.when(f == 0)
    def _():
        xs_ref[HALO:HALO + tm, :] = norm(x_ref[...])
        xs_ref[0:HALO, :] = norm(xp_ref[...])
        nxt = jnp.where(i == pl.num_programs(0) - 1, 0.0, xn_ref[...])
        xs_ref[HALO + tm:ext, :] = norm(nxt)
        acc_ref[...] = jnp.zeros_like(acc_ref)

    xs = xs_ref[...]

    def conv(w_ref, cw_ref, cb_ref):
        u = _dot(xs, w_ref[...])
        u_prev = pltpu.roll(u, 1, 0)[HALO:HALO + tm, :]
        u_next = pltpu.roll(u, ext - 1, 0)[HALO:HALO + tm, :]
        cw = cw_ref[...]
        return cw[0:1, :] * u_prev + cw[1:2, :] * u[HALO:HALO + tm, :] + cw[2:3, :] * u_next + cb_ref[...]

    ca = conv(wa_ref, cwa_ref, cba_ref)
    cb = conv(wb_ref, cwb_ref, cbb_ref)
    act = (ca / (1.0 + jnp.exp(-ca))) * cb
    acc_ref[...] += _dot(act.astype(BF16), wd_ref[...])

    @pl.when(f == pl.num_programs(1) - 1)
    def _():
        y = x_ref[...] + acc_ref[...]
        if final:
            ms = jnp.mean(y * y, axis=-1, keepdims=True)
            y = y * lax.rsqrt(ms + EPS) * gf_ref[...]
        _store_valid_rows(o_ref, y, thr_ref, i, tm)


def _ffn(thr, x, gain, w_up, conv_w, conv_b, w_down, final_gain, *, final):
    rows = x.shape[0]
    nf = D_FF // TF
    per = TM_RES // HALO
    last_halo = rows // HALO - 1
    grid_spec = pltpu.PrefetchScalarGridSpec(
        num_scalar_prefetch=1,
        grid=(rows // TM_RES, nf),
        in_specs=[
            pl.BlockSpec((TM_RES, D_MODEL), lambda i, f, t: (i, 0)),
            pl.BlockSpec((HALO, D_MODEL), lambda i, f, t: (jnp.maximum(i * per - 1, 0), 0)),
            pl.BlockSpec((HALO, D_MODEL), lambda i, f, t: (jnp.minimum((i + 1) * per, last_halo), 0)),
            pl.BlockSpec((1, D_MODEL), lambda i, f, t: (0, 0)),
            pl.BlockSpec((D_MODEL, TF), lambda i, f, t: (0, f)),
            pl.BlockSpec((D_MODEL, TF), lambda i, f, t: (0, nf + f)),
            pl.BlockSpec((3, TF), lambda i, f, t: (0, f)),
            pl.BlockSpec((3, TF), lambda i, f, t: (0, nf + f)),
            pl.BlockSpec((1, TF), lambda i, f, t: (0, f)),
            pl.BlockSpec((1, TF), lambda i, f, t: (0, nf + f)),
            pl.BlockSpec((TF, D_MODEL), lambda i, f, t: (f, 0)),
            pl.BlockSpec((1, D_MODEL), lambda i, f, t: (0, 0)),
        ],
        out_specs=pl.BlockSpec((TM_RES, D_MODEL), lambda i, f, t: (i, 0)),
        scratch_shapes=[
            pltpu.VMEM((TM_RES + 2 * HALO, D_MODEL), BF16),
            pltpu.VMEM((TM_RES, D_MODEL), F32),
        ],
    )
    return pl.pallas_call(
        functools.partial(_ffn_kernel, final=final),
        grid_spec=grid_spec,
        out_shape=jax.ShapeDtypeStruct((rows, D_MODEL), F32),
        compiler_params=pltpu.CompilerParams(
            dimension_semantics=("arbitrary", "arbitrary"), vmem_limit_bytes=VMEM_LIMIT),
        name="ffn",
    )(thr, x, x, x, gain, w_up, w_up, conv_w, conv_w, conv_b, conv_b, w_down, final_gain)


def _pack_sequences(x, meta_tokens):
    b = x.shape[0]
    pad = jnp.zeros((b, PAD_ROWS, D_MODEL), x.dtype)
    meta = jnp.broadcast_to(meta_tokens.astype(x.dtype)[None], (b, N_META, D_MODEL))
    return jnp.concatenate([pad, meta, x], axis=1).reshape(-1, D_MODEL)


def _in_weights(w):
    q = w[:, :ATTN_WIDTH] * (HEAD_DIM ** -0.5)
    k = w[:, ATTN_WIDTH:ATTN_WIDTH + KV_WIDTH].reshape(D_MODEL, ATTN_KV_HEADS, 1, HEAD_DIM)
    v = w[:, ATTN_WIDTH + KV_WIDTH:ATTN_WIDTH + 2 * KV_WIDTH].reshape(D_MODEL, ATTN_KV_HEADS, 1, HEAD_DIM)
    kk = jnp.broadcast_to(k, (D_MODEL, ATTN_KV_HEADS, 2, HEAD_DIM)).reshape(D_MODEL, KK_WIDTH)
    vv = jnp.broadcast_to(v, (D_MODEL, ATTN_KV_HEADS, 2, HEAD_DIM)).reshape(D_MODEL, KK_WIDTH)
    hg = w[:, ATTN_WIDTH + 2 * KV_WIDTH:]
    return jnp.concatenate([q, kk, vv, hg], axis=1).astype(BF16)


def _lower_bound_params(lb_param, layer):
    c = jnp.cumsum(jax.nn.softmax(lb_param.astype(F32), axis=1), axis=1)
    lb = (c - c[:, :1])[:, layer]
    rows = [jnp.log(lb[0]), jnp.log1p(-lb[0]), 1.0 - lb[0], jnp.log(lb[1]), jnp.log1p(-lb[1]), 1.0 - lb[1]]
    rows += [jnp.zeros_like(lb[0])] * 2
    return jnp.stack(rows, axis=0)


def kernel(x_prompt, x_sample, meta_tokens, mix_norm, w_in, attn_sink, attn_out_norm, hgrn_lower_bounds,
           hgrn_out_norm, w_out, ffn_norm, w_up, conv_w, conv_b, w_down, final_norm):
    bp, sp, _ = x_prompt.shape
    bs, ss, _ = x_sample.shape
    depth = w_in.shape[0]
    lay = _make_layout(bp, sp, bs, ss)
    tables = _block_tables(lay)
    thr = jnp.asarray(tables[0])
    ta, tb = (jnp.asarray(t) for t in _attn_bias_tables())
    consts = _hgrn_consts()

    rows_s = bs * lay.nb_s * BLK
    gap = lay.base_p * BLK - rows_s
    tail = lay.n_blocks * BLK - (lay.base_p + bp * lay.nb_p) * BLK
    parts = [_pack_sequences(x_sample, meta_tokens)]
    if gap:
        parts.append(jnp.zeros((gap, D_MODEL), F32))
    parts.append(_pack_sequences(x_prompt, meta_tokens))
    if tail:
        parts.append(jnp.zeros((tail, D_MODEL), F32))
    h = jnp.concatenate(parts, axis=0)
    rows = h.shape[0]

    row2 = lambda t: t.reshape(1, -1).astype(F32)
    for l in range(depth):
        qkv, hg = _inproj(h, row2(mix_norm[l]), _in_weights(w_in[l]))
        attn = _attention(qkv, tables, ta, tb, attn_sink[l].astype(F32), row2(attn_out_norm[l]))
        lbp = _lower_bound_params(hgrn_lower_bounds, l)
        gn = row2(hgrn_out_norm[l])
        rec = jnp.zeros((rows, HG_WIDTH), BF16)
        rec = _hgrn_group(hg, rec, lbp, gn, consts, base_blk=0, batch=bs, nb_seq=lay.nb_s)
        rec = _hgrn_group(hg, rec, lbp, gn, consts, base_blk=lay.base_p, batch=bp, nb_seq=lay.nb_p)
        wo = w_out[l].astype(BF16)
        h = _outproj(thr, attn, rec, h, wo[:ATTN_WIDTH], wo[ATTN_WIDTH:])
        h = _ffn(thr, h, row2(ffn_norm[l]), w_up[l].astype(BF16), conv_w[l].astype(F32),
                 row2(conv_b[l]), w_down[l].astype(BF16), row2(final_norm), final=(l == depth - 1))

    y_s = h[:rows_s].reshape(bs, lay.nb_s * BLK, D_MODEL)[:, BLK:]
    y_p = h[lay.base_p * BLK:(lay.base_p + bp * lay.nb_p) * BLK].reshape(bp, lay.nb_p * BLK, D_MODEL)[:, BLK:]
    return (y_p, y_s)
```

```python
import functools
from typing import NamedTuple

import numpy as np
import jax
import jax.numpy as jnp
from jax import lax
from jax.experimental import pallas as pl
from jax.experimental.pallas import tpu as pltpu

F32 = jnp.float32
BF16 = jnp.bfloat16

D_MODEL = 2048
N_META = 16
HEAD_DIM = 64
ATTN_WIDTH = 1024
ATTN_HEADS = 16
ATTN_KV_HEADS = 4
KV_WIDTH = ATTN_KV_HEADS * HEAD_DIM
BLK = 128
PAD_ROWS = BLK - N_META
HG_HEADS = 8
HG_DIM = 128
HG_WIDTH = HG_HEADS * HG_DIM
HG_PER_STEP = 4
D_FF = 5632
EPS = 1e-6
NEG_INF = -1e30
LOG2_E = 1.4426950408889634
KK_WIDTH = ATTN_KV_HEADS * 128
QKV_COLS = ATTN_WIDTH + 2 * KK_WIDTH
IN_COLS2 = QKV_COLS + 5 * HG_WIDTH

TM = 1024
TM_RES = 512
TN_IN = 1024
TF = 512
HALO = 8
VMEM_LIMIT = 56 * 1024 * 1024


def _dot(a, b):
    return jnp.dot(a, b, preferred_element_type=F32)


def _dot_nt(a, b):
    return lax.dot_general(a, b, (((1,), (1,)), ((), ())), preferred_element_type=F32)


class _Layout(NamedTuple):
    nb_s: int
    nb_p: int
    base_p: int
    n_blocks: int
    seqs: tuple


def _make_layout(bp, sp, bs, ss):
    nb_s, nb_p = ss // BLK + 1, sp // BLK + 1
    base_p = -(-(bs * nb_s) // nb_p) * nb_p
    used = base_p + bp * nb_p
    per_tile = TM // BLK
    n_blocks = -(-used // per_tile) * per_tile
    seqs = tuple((b * nb_s, nb_s) for b in range(bs)) + tuple((base_p + b * nb_p, nb_p) for b in range(bp))
    return _Layout(nb_s, nb_p, base_p, n_blocks, seqs)


def _block_tables(lay):
    n = lay.n_blocks
    thr = np.full((n,), BLK, np.int32)
    prev = np.arange(n, dtype=np.int32)
    nxt = np.arange(n, dtype=np.int32)
    meta = np.arange(n, dtype=np.int32)
    flags = np.zeros((n,), np.int32)
    for first, cnt in lay.seqs:
        for i in range(cnt):
            blk = first + i
            thr[blk] = PAD_ROWS if i == 0 else 0
            prev[blk] = max(blk - 1, first)
            nxt[blk] = min(blk + 1, first + cnt - 1)
            meta[blk] = first
            flags[blk] = (1 if i >= 2 else 0) | (2 if i >= 1 else 0) | (4 if i + 1 <= cnt - 1 else 0)
    return thr, prev, nxt, meta, flags


def _inproj_kernel(x_ref, g_ref, w_ref, o_ref, xn_ref):
    @pl.when(pl.program_id(1) == 0)
    def _():
        x = x_ref[...]
        ms = jnp.mean(x * x, axis=-1, keepdims=True)
        xn_ref[...] = (x * lax.rsqrt(ms + EPS) * g_ref[...]).astype(BF16)

    o_ref[...] = _dot(xn_ref[...], w_ref[...]).astype(BF16)


def _inproj(x, gain, w):
    rows = x.shape[0]
    return pl.pallas_call(
        _inproj_kernel,
        grid=(rows // TM, IN_COLS2 // TN_IN),
        in_specs=[
            pl.BlockSpec((TM, D_MODEL), lambda i, j: (i, 0)),
            pl.BlockSpec((1, D_MODEL), lambda i, j: (0, 0)),
            pl.BlockSpec((D_MODEL, TN_IN), lambda i, j: (0, j)),
        ],
        out_specs=pl.BlockSpec((TM, TN_IN), lambda i, j: (i, j)),
        out_shape=jax.ShapeDtypeStruct((rows, IN_COLS2), BF16),
        scratch_shapes=[pltpu.VMEM((TM, D_MODEL), BF16)],
        compiler_params=pltpu.CompilerParams(
            dimension_semantics=("arbitrary", "arbitrary"), vmem_limit_bytes=VMEM_LIMIT),
        name="inproj",
    )(x, gain, w)


HALF = BLK // 2
N_KEYS = 3 * BLK
KEY_PAD = N_KEYS - (2 * BLK + HALF + N_META)


def _attn_bias_tables():
    slopes = (2.0 ** (-8.0 * np.arange(1, ATTN_HEADS + 1) / ATTN_HEADS)).astype(np.float32)
    full = np.arange(BLK)
    dists, oks, groups = [], [], []
    for i, prev_j, next_j in ((np.arange(HALF), full, full[:HALF]), (HALF + np.arange(HALF), full[HALF:], full)):
        i = i[:, None]
        d_prev = BLK + i - prev_j[None]
        d_cur = np.abs(i - full[None])
        d_next = BLK + next_j[None] - i
        tail = np.zeros((HALF, N_META + KEY_PAD), np.int64)
        dists.append(np.concatenate([d_prev, d_cur, d_next, tail], axis=1))
        ok_tail = np.concatenate([np.ones((HALF, N_META), bool), np.zeros((HALF, KEY_PAD), bool)], axis=1)
        oks.append(np.concatenate([d_prev <= BLK, d_cur <= BLK, d_next <= BLK, ok_tail], axis=1))
        groups.append(np.concatenate([np.full(len(prev_j), 0), np.full(BLK, 1), np.full(len(next_j), 2),
                                      np.full(N_META + KEY_PAD, 3)]))
    dist = np.concatenate(dists, axis=0).astype(np.float32)
    ok = np.concatenate(oks, axis=0)
    group = np.stack(groups, axis=0).repeat(HALF, axis=0)
    flags = np.arange(8)[:, None, None]
    group_ok = np.where(group[None] == 3, True, ((flags >> np.minimum(group[None], 2)) & 1) > 0)
    valid = jnp.asarray(group_ok & ok[None])
    base = -(jnp.asarray(slopes)[:, None, None] * jnp.asarray(dist)[None]) * LOG2_E
    return jnp.where(valid[:, None], base[None], NEG_INF).astype(F32)


def _attn_kernel(prev_t, next_t, meta_t, flag_t, q_ref, kp_ref, kc_ref, kn_ref, km_ref,
                 t_ref, sink_ref, gain_ref, o_ref, acc_ref, kk_ref, vv_ref):
    del prev_t, next_t, meta_t
    lo = lax.broadcasted_iota(jnp.int32, (N_KEYS, BLK), 1) < HEAD_DIM
    lo_q = lax.broadcasted_iota(jnp.int32, (HALF, BLK), 1) < HEAD_DIM
    zero = jnp.zeros((), BF16)
    pad = jnp.zeros((KEY_PAD, BLK), BF16)
    ones_lo = jnp.where(lo, 1.0, 0.0).astype(BF16)
    for kh in range(ATTN_KV_HEADS):
        for half in range(2):
            vv_ref[kh, half, :N_KEYS, BLK:] = ones_lo
            vv_ref[kh, half, N_KEYS:, BLK:] = 1.0 - ones_lo

    def stack_keys(kh):
        kcols = slice(kh * BLK, (kh + 1) * BLK)
        vcols = slice(KK_WIDTH + kh * BLK, KK_WIDTH + (kh + 1) * BLK)
        for half in range(2):
            def keys(c):
                if half == 0:
                    parts = [kp_ref[:, c], kc_ref[:, c], kn_ref[:HALF, c]]
                else:
                    parts = [kp_ref[HALF:, c], kc_ref[:, c], kn_ref[:, c]]
                return jnp.concatenate(parts + [km_ref[PAD_ROWS:, c], pad], axis=0)

            kcat, vcat = keys(kcols), keys(vcols)
            kk_ref[kh, half, :N_KEYS, :] = jnp.where(lo, kcat, zero)
            kk_ref[kh, half, N_KEYS:, :] = jnp.where(lo, zero, kcat)
            vv_ref[kh, half, :N_KEYS, :BLK] = jnp.where(lo, vcat, zero)
            vv_ref[kh, half, N_KEYS:, :BLK] = jnp.where(lo, zero, vcat)

    def units(g):
        kh, j = divmod(g, 2)
        return [(kh, j, half) for half in range(2)]

    n_groups = 2 * ATTN_KV_HEADS

    def scores(kh, j, half):
        rows = slice(half * HALF, (half + 1) * HALF)
        col = (2 * kh + j) * BLK
        return _dot_nt(q_ref[rows, col:col + BLK], kk_ref[kh, half])

    def softmax(s, kh, j, half):
        rows = slice(half * HALF, (half + 1) * HALF)
        h_e = 4 * kh + 2 * j
        s_e = s[:, :N_KEYS] + t_ref[0, h_e, rows, :]
        s_o = s[:, N_KEYS:] + t_ref[0, h_e + 1, rows, :]
        m_e = jnp.maximum(jnp.max(s_e, axis=-1, keepdims=True), sink_ref[h_e])
        m_o = jnp.maximum(jnp.max(s_o, axis=-1, keepdims=True), sink_ref[h_e + 1])
        p = jnp.concatenate([jnp.exp2(s_e - m_e), jnp.exp2(s_o - m_o)], axis=1).astype(BF16)
        sink_p = jnp.where(lo_q, jnp.exp2(sink_ref[h_e] - m_e), jnp.exp2(sink_ref[h_e + 1] - m_o))
        return p, sink_p

    def weighted_values(p, sink_p, kh, j, half):
        rows = slice(half * HALF, (half + 1) * HALF)
        col = (2 * kh + j) * BLK
        o4 = _dot(p, vv_ref[kh, half])
        acc_ref[rows, col:col + BLK] = o4[:, :BLK] / (o4[:, BLK:] + sink_p)

    for kh in range(ATTN_KV_HEADS):
        stack_keys(kh)
    pending = None
    for g in range(n_groups + 1):
        s_now = [scores(*u) for u in units(g)] if g < n_groups else None
        if pending is not None:
            probs = [softmax(s, *u) for s, u in zip(pending, units(g - 1))]
            for (p, sink_p), u in zip(probs, units(g - 1)):
                weighted_values(p, sink_p, *u)
        pending = s_now

    o = acc_ref[...]
    ms = jnp.mean(o * o, axis=-1, keepdims=True)
    o_ref[...] = (o * lax.rsqrt(ms + EPS) * gain_ref[...]).astype(BF16)


def _attention(qkv, tables, bias, sink, gain):
    _, prev, nxt, meta, flags = tables
    n_blocks = qkv.shape[0] // BLK
    kv_spec = lambda f: pl.BlockSpec((BLK, 2 * KK_WIDTH), f)
    grid_spec = pltpu.PrefetchScalarGridSpec(
        num_scalar_prefetch=4,
        grid=(n_blocks,),
        in_specs=[
            pl.BlockSpec((BLK, ATTN_WIDTH), lambda i, p, n, m, f: (i, 0)),
            kv_spec(lambda i, p, n, m, f: (p[i], 1)),
            kv_spec(lambda i, p, n, m, f: (i, 1)),
            kv_spec(lambda i, p, n, m, f: (n[i], 1)),
            kv_spec(lambda i, p, n, m, f: (m[i], 1)),
            pl.BlockSpec((1, ATTN_HEADS, BLK, N_KEYS), lambda i, p, n, m, f: (f[i], 0, 0, 0)),
            pl.BlockSpec(memory_space=pltpu.SMEM),
            pl.BlockSpec((1, ATTN_WIDTH), lambda i, p, n, m, f: (0, 0)),
        ],
        out_specs=pl.BlockSpec((BLK, ATTN_WIDTH), lambda i, p, n, m, f: (i, 0)),
        scratch_shapes=[
            pltpu.VMEM((BLK, ATTN_WIDTH), F32),
            pltpu.VMEM((ATTN_KV_HEADS, 2, 2 * N_KEYS, BLK), BF16),
            pltpu.VMEM((ATTN_KV_HEADS, 2, 2 * N_KEYS, 2 * BLK), BF16),
        ],
    )
    return pl.pallas_call(
        _attn_kernel,
        grid_spec=grid_spec,
        out_shape=jax.ShapeDtypeStruct((qkv.shape[0], ATTN_WIDTH), BF16),
        compiler_params=pltpu.CompilerParams(
            dimension_semantics=("arbitrary",), vmem_limit_bytes=VMEM_LIMIT),
        name="attention",
    )(jnp.asarray(prev), jnp.asarray(nxt), jnp.asarray(meta), jnp.asarray(flags),
      qkv, qkv, qkv, qkv, qkv, bias, sink, gain)


_LEVELS = (64, 32, 16, 8, 4, 2, 1)


def _hgrn_consts():
    t = np.arange(BLK)[:, None]
    s = np.arange(BLK)[None, :]
    lmat = (s <= t).astype(np.float32)
    umat = (s >= t).astype(np.float32)
    x = t ^ s
    lv = np.where(x == 0, -1, np.floor(np.log2(np.maximum(x, 1)))).astype(np.int32)
    return jnp.asarray(lmat, BF16), jnp.asarray(umat, BF16), jnp.asarray(lv)


def _level_reference(ac_ref, c, rev):
    off = c - 1 + (1 if rev else 0)
    if c == 4:
        pieces = [jnp.broadcast_to(ac_ref[bs + off:bs + off + 1, :], (8, HG_DIM)) for bs in range(0, BLK, 8)]
        return jnp.concatenate(pieces, axis=0)
    assert c == 2
    sub = lax.broadcasted_iota(jnp.int32, (8, HG_DIM), 0)
    pieces = []
    for bs in range(0, BLK, 8):
        r1 = jnp.broadcast_to(ac_ref[bs + off:bs + off + 1, :], (8, HG_DIM))
        r2 = jnp.broadcast_to(ac_ref[bs + 4 + off:bs + 4 + off + 1, :], (8, HG_DIM))
        pieces.append(jnp.where(sub < 4, r1, r2))
    return jnp.concatenate(pieces, axis=0)


def _upper_level_operands(q, k, ac, ac_ref, c, rev):
    zeros = jnp.zeros((c, HG_DIM), F32)
    q_parts, k_parts = [], []
    for bs in range(0, BLK, 2 * c):
        lo_rows, hi_rows = slice(bs, bs + c), slice(bs + c, bs + 2 * c)
        q_rows, k_rows = (lo_rows, hi_rows) if rev else (hi_rows, lo_rows)
        r = bs + c if rev else bs + c - 1
        ref = ac_ref[r:r + 1, :]
        qb = q[q_rows] * jnp.exp2(ac[q_rows] - ref)
        kb = k[k_rows] * jnp.exp2(ref - ac[k_rows])
        q_parts += [qb, zeros] if rev else [zeros, qb]
        k_parts += [zeros, kb] if rev else [kb, zeros]
    return jnp.concatenate(q_parts, axis=0).astype(BF16), jnp.concatenate(k_parts, axis=0).astype(BF16)


def _hgrn_chunk(qs, zs, vs, first_row, lbps, cmat, lv, ac_refs, st_refs, rev):
    n = len(qs)
    off = 3 if rev else 0
    row = lax.broadcasted_iota(jnp.int32, (BLK, HG_DIM), 0)
    valid = row >= first_row

    lfs, ks, acs = [], [], []
    for q, z, lbp, ac_ref in zip(qs, zs, lbps, ac_refs):
        log_lb, log_1m_lb, one_m_lb = lbp[off:off + 1, :], lbp[off + 1:off + 2, :], lbp[off + 2:off + 3, :]
        log_sig = jnp.minimum(z, 0.0) - jnp.log(1.0 + jnp.exp(-jnp.abs(z)))
        bv = log_1m_lb + log_sig
        lf = jnp.maximum(log_lb, bv) + jnp.log(1.0 + jnp.exp(-jnp.abs(log_lb - bv)))
        k = one_m_lb * jnp.exp(log_sig - z)
        lf = jnp.where(valid, lf * LOG2_E, 0.0)
        k = jnp.where(valid, k, 0.0)
        hi = lf.astype(BF16)
        lo = (lf - hi.astype(F32)).astype(BF16)
        ac2 = _dot(cmat, jnp.concatenate([hi, lo], axis=1))
        ac = ac2[:, :HG_DIM] + ac2[:, HG_DIM:]
        ac_ref[...] = ac
        lfs.append(lf)
        ks.append(k)
        acs.append(ac)

    ps = [None] * n
    for c in _LEVELS:
        q_rows = ((row & c) == 0) if rev else ((row & c) != 0)
        for i in range(n):
            q, k, ac = qs[i], ks[i], acs[i]
            if c >= 8:
                qt, kt = _upper_level_operands(q, k, ac, ac_refs[i], c, rev)
            else:
                if c == 1:
                    w = jnp.where(q_rows, jnp.exp2(lfs[i]), 1.0)
                else:
                    d = ac - _level_reference(ac_refs[i], c, rev)
                    w = jnp.exp2(jnp.where(q_rows, d, -d))
                qt = jnp.where(q_rows, q * w, 0.0).astype(BF16)
                kt = jnp.where(q_rows, 0.0, k * w).astype(BF16)
            pc = _dot_nt(qt, kt)
            ps[i] = pc if ps[i] is None else jnp.where(lv == int(np.log2(c)), pc, ps[i])

    outs = []
    for i in range(n):
        p = jnp.where(lv == -1, jnp.sum(qs[i] * ks[i], axis=-1, keepdims=True), ps[i])
        st = st_refs[i][...]
        outs.append(_dot(p.astype(BF16), vs[i]) + _dot_nt((qs[i] * jnp.exp2(acs[i])).astype(BF16), st.astype(BF16)))
    for i in range(n):
        edge = ac_refs[i][0:1, :] if rev else ac_refs[i][BLK - 1:BLK, :]
        ke = (ks[i] * jnp.exp2(edge - acs[i])).astype(BF16)
        st_refs[i][...] = st_refs[i][...] * jnp.exp2(edge) + _dot(vs[i].T, ke)
    return outs


def _hgrn_kernel(q_ref, zf_ref, zb_ref, v_ref, g_ref, lbp_ref, gn_ref, lmat_ref, umat_ref, lv_ref,
                 alias_ref, o_ref, st_ref, obuf_ref, ac_ref, *, nrb, tb):
    del alias_ref
    ph = pl.program_id(2)
    rb = pl.program_id(3)
    nch = tb // BLK
    lv = lv_ref[...]

    @pl.when(rb == 0)
    def _():
        st_ref[...] = jnp.zeros_like(st_ref)

    def sweep(rev):
        z_ref = zb_ref if rev else zf_ref
        cmat = umat_ref[...] if rev else lmat_ref[...]
        rb_seq = (nrb - 1 - rb) if rev else rb

        def body(ci, carry):
            cj = (nch - 1 - ci) if rev else ci
            r0 = pl.multiple_of(cj * BLK, BLK)
            rows = pl.ds(r0, BLK)
            first_row = jnp.where(rb_seq * nch + cj > 0, 0, PAD_ROWS)
            seq_rows = pl.ds(pl.multiple_of(rb_seq * tb + r0, BLK), BLK)
            heads = range(HG_PER_STEP)
            col = [slice(hh * HG_DIM, (hh + 1) * HG_DIM) for hh in heads]
            outs = _hgrn_chunk([q_ref[rows, c].astype(F32) for c in col],
                               [z_ref[rows, c].astype(F32) for c in col],
                               [v_ref[rows, c] for c in col], first_row,
                               [lbp_ref.at[:, c] for c in col], cmat, lv,
                               [ac_ref.at[hh] for hh in heads], [st_ref.at[hh] for hh in heads], rev)
            for cols, o in zip(col, outs):
                if rev:
                    obuf_ref[seq_rows, cols] = o
                else:
                    o = o + obuf_ref[seq_rows, cols]
                    ms = jnp.mean(o * o, axis=-1, keepdims=True)
                    y = o * lax.rsqrt(ms + EPS) * gn_ref[:, cols]
                    g = g_ref[rows, cols].astype(F32)
                    o_ref[rows, cols] = (y * (g / (1.0 + jnp.exp(-g)))).astype(BF16)
            return carry

        lax.fori_loop(0, nch, body, 0)

    @pl.when(ph == 0)
    def _():
        sweep(True)

    @pl.when(ph == 1)
    def _():
        sweep(False)


def _hgrn_group(hg, rec, lbp, gn, consts, *, base_blk, batch, nb_seq):
    kmax = nb_seq if nb_seq * BLK <= 4096 else max(k for k in range(1, 5) if nb_seq % k == 0)
    tb = kmax * BLK
    nrb = nb_seq // kmax
    assert (base_blk * BLK) % tb == 0
    base = base_blk * BLK // tb
    lmat, umat, lv = consts

    def rows_in(b, h, ph, rb):
        return base + b * nrb + jnp.where(ph == 0, nrb - 1 - rb, rb)

    def rows_fwd_only(b, h, ph, rb):
        return base + b * nrb + jnp.where(ph == 0, 0, rb)

    def rows_bwd_only(b, h, ph, rb):
        return base + b * nrb + jnp.where(ph == 0, nrb - 1 - rb, 0)

    width = HG_PER_STEP * HG_DIM
    groups = HG_HEADS // HG_PER_STEP
    col_base = QKV_COLS // width
    blk = lambda rows_fn, col0: pl.BlockSpec(
        (tb, width), lambda b, h, ph, rb: (rows_fn(b, h, ph, rb), col0 + h))
    const = lambda shape: pl.BlockSpec(shape, lambda b, h, ph, rb: (0, 0))
    seq_rows = nb_seq * BLK
    return pl.pallas_call(
        functools.partial(_hgrn_kernel, nrb=nrb, tb=tb),
        grid=(batch, groups, 2, nrb),
        in_specs=[
            blk(rows_in, col_base),
            blk(rows_fwd_only, col_base + groups),
            blk(rows_bwd_only, col_base + 2 * groups),
            blk(rows_in, col_base + 3 * groups),
            blk(rows_fwd_only, col_base + 4 * groups),
            pl.BlockSpec((8, width), lambda b, h, ph, rb: (0, h)),
            pl.BlockSpec((1, width), lambda b, h, ph, rb: (0, h)),
            const((BLK, BLK)), const((BLK, BLK)), const((BLK, BLK)),
            pl.BlockSpec(memory_space=pl.ANY),
        ],
        out_specs=blk(rows_fwd_only, 0),
        out_shape=jax.ShapeDtypeStruct(rec.shape, rec.dtype),
        input_output_aliases={10: 0},
        scratch_shapes=[
            pltpu.VMEM((HG_PER_STEP, HG_DIM, HG_DIM), F32),
            pltpu.VMEM((seq_rows, width), F32),
            pltpu.VMEM((HG_PER_STEP, BLK, HG_DIM), F32),
        ],
        compiler_params=pltpu.CompilerParams(
            dimension_semantics=("arbitrary",) * 4, vmem_limit_bytes=VMEM_LIMIT),
        name="hgrn",
    )(hg, hg, hg, hg, hg, lbp, gn, lmat, umat, lv, rec)


def _store_valid_rows(o_ref, y, thr_ref, tile, rows_per_tile):
    row = lax.broadcasted_iota(jnp.int32, (BLK, y.shape[1]), 0)
    for kb in range(rows_per_tile // BLK):
        t = thr_ref[tile * (rows_per_tile // BLK) + kb]
        o_ref[kb * BLK:(kb + 1) * BLK, :] = jnp.where(row >= t, y[kb * BLK:(kb + 1) * BLK, :], 0.0)


def _outproj_kernel(thr_ref, a_ref, r_ref, x_ref, wa_ref, wr_ref, o_ref):
    y = x_ref[...] + _dot(a_ref[...], wa_ref[...]) + _dot(r_ref[...], wr_ref[...])
    _store_valid_rows(o_ref, y, thr_ref, pl.program_id(0), TM_RES)


def _outproj(thr, attn, rec, x, wa, wr):
    rows = x.shape[0]
    grid_spec = pltpu.PrefetchScalarGridSpec(
        num_scalar_prefetch=1,
        grid=(rows // TM_RES,),
        in_specs=[
            pl.BlockSpec((TM_RES, ATTN_WIDTH), lambda i, t: (i, 0)),
            pl.BlockSpec((TM_RES, HG_WIDTH), lambda i, t: (i, 0)),
            pl.BlockSpec((TM_RES, D_MODEL), lambda i, t: (i, 0)),
            pl.BlockSpec((ATTN_WIDTH, D_MODEL), lambda i, t: (0, 0)),
            pl.BlockSpec((HG_WIDTH, D_MODEL), lambda i, t: (0, 0)),
        ],
        out_specs=pl.BlockSpec((TM_RES, D_MODEL), lambda i, t: (i, 0)),
    )
    return pl.pallas_call(
        _outproj_kernel,
        grid_spec=grid_spec,
        out_shape=jax.ShapeDtypeStruct((rows, D_MODEL), F32),
        compiler_params=pltpu.CompilerParams(
            dimension_semantics=("arbitrary",), vmem_limit_bytes=VMEM_LIMIT),
        name="outproj",
    )(thr, attn, rec, x, wa, wr)


def _ffn_kernel(thr_ref, x_ref, xp_ref, xn_ref, g_ref, wa_ref, wb_ref, cwa_ref, cwb_ref, cba_ref, cbb_ref,
                wd_ref, gf_ref, o_ref, xs_ref, acc_ref, *, final):
    i = pl.program_id(0)
    f = pl.program_id(1)
    tm = x_ref.shape[0]
    ext = tm + 2 * HALO

    def norm(x):
        ms = jnp.mean(x * x, axis=-1, keepdims=True)
        return (x * lax.rsqrt(ms + EPS) * g_ref[...]).astype(BF16)

    @pl.when(f == 0)
    def _():
        xs_ref[HALO:HALO + tm, :] = norm(x_ref[...])
        xs_ref[0:HALO, :] = norm(xp_ref[...])
        nxt = jnp.where(i == pl.num_programs(0) - 1, 0.0, xn_ref[...])
        xs_ref[HALO + tm:ext, :] = norm(nxt)
        acc_ref[...] = jnp.zeros_like(acc_ref)

    xs = xs_ref[...]

    def conv(w_ref, cw_ref, cb_ref):
        u = _dot(xs, w_ref[...])
        u_prev = pltpu.roll(u, 1, 0)[HALO:HALO + tm, :]
        u_next = pltpu.roll(u, ext - 1, 0)[HALO:HALO + tm, :]
        cw = cw_ref[...]
        return cw[0:1, :] * u_prev + cw[1:2, :] * u[HALO:HALO + tm, :] + cw[2:3, :] * u_next + cb_ref[...]

    ca = conv(wa_ref, cwa_ref, cba_ref)
    cb = conv(wb_ref, cwb_ref, cbb_ref)
    act = (ca / (1.0 + jnp.exp(-ca))) * cb
    acc_ref[...] += _dot(act.astype(BF16), wd_ref[...])

    @pl.when(f == pl.num_programs(1) - 1)
    def _():
        y = x_ref[...] + acc_ref[...]
        if final:
            ms = jnp.mean(y * y, axis=-1, keepdims=True)
            y = y * lax.rsqrt(ms + EPS) * gf_ref[...]
        _store_valid_rows(o_ref, y, thr_ref, i, tm)


def _ffn(thr, x, gain, w_up, conv_w, conv_b, w_down, final_gain, *, final):
    rows = x.shape[0]
    nf = D_FF // TF
    per = TM_RES // HALO
    last_halo = rows // HALO - 1
    grid_spec = pltpu.PrefetchScalarGridSpec(
        num_scalar_prefetch=1,
        grid=(rows // TM_RES, nf),
        in_specs=[
            pl.BlockSpec((TM_RES, D_MODEL), lambda i, f, t: (i, 0)),
            pl.BlockSpec((HALO, D_MODEL), lambda i, f, t: (jnp.maximum(i * per - 1, 0), 0)),
            pl.BlockSpec((HALO, D_MODEL), lambda i, f, t: (jnp.minimum((i + 1) * per, last_halo), 0)),
            pl.BlockSpec((1, D_MODEL), lambda i, f, t: (0, 0)),
            pl.BlockSpec((D_MODEL, TF), lambda i, f, t: (0, f)),
            pl.BlockSpec((D_MODEL, TF), lambda i, f, t: (0, nf + f)),
            pl.BlockSpec((3, TF), lambda i, f, t: (0, f)),
            pl.BlockSpec((3, TF), lambda i, f, t: (0, nf + f)),
            pl.BlockSpec((1, TF), lambda i, f, t: (0, f)),
            pl.BlockSpec((1, TF), lambda i, f, t: (0, nf + f)),
            pl.BlockSpec((TF, D_MODEL), lambda i, f, t: (f, 0)),
            pl.BlockSpec((1, D_MODEL), lambda i, f, t: (0, 0)),
        ],
        out_specs=pl.BlockSpec((TM_RES, D_MODEL), lambda i, f, t: (i, 0)),
        scratch_shapes=[
            pltpu.VMEM((TM_RES + 2 * HALO, D_MODEL), BF16),
            pltpu.VMEM((TM_RES, D_MODEL), F32),
        ],
    )
    return pl.pallas_call(
        functools.partial(_ffn_kernel, final=final),
        grid_spec=grid_spec,
        out_shape=jax.ShapeDtypeStruct((rows, D_MODEL), F32),
        compiler_params=pltpu.CompilerParams(
            dimension_semantics=("arbitrary", "arbitrary"), vmem_limit_bytes=VMEM_LIMIT),
        name="ffn",
    )(thr, x, x, x, gain, w_up, w_up, conv_w, conv_w, conv_b, conv_b, w_down, final_gain)


def _pack_sequences(x, meta_tokens):
    b = x.shape[0]
    pad = jnp.zeros((b, PAD_ROWS, D_MODEL), x.dtype)
    meta = jnp.broadcast_to(meta_tokens.astype(x.dtype)[None], (b, N_META, D_MODEL))
    return jnp.concatenate([pad, meta, x], axis=1).reshape(-1, D_MODEL)


def _in_weights(w):
    q = w[:, :ATTN_WIDTH] * (HEAD_DIM ** -0.5 * LOG2_E)
    k = w[:, ATTN_WIDTH:ATTN_WIDTH + KV_WIDTH].reshape(D_MODEL, ATTN_KV_HEADS, 1, HEAD_DIM)
    v = w[:, ATTN_WIDTH + KV_WIDTH:ATTN_WIDTH + 2 * KV_WIDTH].reshape(D_MODEL, ATTN_KV_HEADS, 1, HEAD_DIM)
    kk = jnp.broadcast_to(k, (D_MODEL, ATTN_KV_HEADS, 2, HEAD_DIM)).reshape(D_MODEL, KK_WIDTH)
    vv = jnp.broadcast_to(v, (D_MODEL, ATTN_KV_HEADS, 2, HEAD_DIM)).reshape(D_MODEL, KK_WIDTH)
    hg = w[:, ATTN_WIDTH + 2 * KV_WIDTH:]
    return jnp.concatenate([q, kk, vv, hg], axis=1).astype(BF16)


def _lower_bound_params(lb_param, layer):
    c = jnp.cumsum(jax.nn.softmax(lb_param.astype(F32), axis=1), axis=1)
    lb = (c - c[:, :1])[:, layer]
    rows = [jnp.log(lb[0]), jnp.log1p(-lb[0]), 1.0 - lb[0], jnp.log(lb[1]), jnp.log1p(-lb[1]), 1.0 - lb[1]]
    rows += [jnp.zeros_like(lb[0])] * 2
    return jnp.stack(rows, axis=0)


def kernel(x_prompt, x_sample, meta_tokens, mix_norm, w_in, attn_sink, attn_out_norm, hgrn_lower_bounds,
           hgrn_out_norm, w_out, ffn_norm, w_up, conv_w, conv_b, w_down, final_norm):
    bp, sp, _ = x_prompt.shape
    bs, ss, _ = x_sample.shape
    depth = w_in.shape[0]
    lay = _make_layout(bp, sp, bs, ss)
    tables = _block_tables(lay)
    thr = jnp.asarray(tables[0])
    bias = _attn_bias_tables()
    consts = _hgrn_consts()

    rows_s = bs * lay.nb_s * BLK
    gap = lay.base_p * BLK - rows_s
    tail = lay.n_blocks * BLK - (lay.base_p + bp * lay.nb_p) * BLK
    parts = [_pack_sequences(x_sample, meta_tokens)]
    if gap:
        parts.append(jnp.zeros((gap, D_MODEL), F32))
    parts.append(_pack_sequences(x_prompt, meta_tokens))
    if tail:
        parts.append(jnp.zeros((tail, D_MODEL), F32))
    h = jnp.concatenate(parts, axis=0)
    rows = h.shape[0]

    row2 = lambda t: t.reshape(1, -1).astype(F32)
    for l in range(depth):
        proj = _inproj(h, row2(mix_norm[l]), _in_weights(w_in[l]))
        hg = proj
        attn = _attention(proj, tables, bias, attn_sink[l].astype(F32) * LOG2_E, row2(attn_out_norm[l]))
        lbp = _lower_bound_params(hgrn_lower_bounds, l)
        gn = row2(hgrn_out_norm[l])
        rec = jnp.zeros((rows, HG_WIDTH), BF16)
        rec = _hgrn_group(hg, rec, lbp, gn, consts, base_blk=0, batch=bs, nb_seq=lay.nb_s)
        rec = _hgrn_group(hg, rec, lbp, gn, consts, base_blk=lay.base_p, batch=bp, nb_seq=lay.nb_p)
        wo = w_out[l].astype(BF16)
        h = _outproj(thr, attn, rec, h, wo[:ATTN_WIDTH], wo[ATTN_WIDTH:])
        h = _ffn(thr, h, row2(ffn_norm[l]), w_up[l].astype(BF16), conv_w[l].astype(F32),
                 row2(conv_b[l]), w_down[l].astype(BF16), row2(final_norm), final=(l == depth - 1))

    y_s = h[:rows_s].reshape(bs, lay.nb_s * BLK, D_MODEL)[:, BLK:]
    y_p = h[lay.base_p * BLK:(lay.base_p + bp * lay.nb_p) * BLK].reshape(bp, lay.nb_p * BLK, D_MODEL)[:, BLK:]
    return (y_p, y_s)
```

```python
import functools
from typing import NamedTuple

import numpy as np
import jax
import jax.numpy as jnp
from jax import lax
from jax.experimental import pallas as pl
from jax.experimental.pallas import tpu as pltpu

F32 = jnp.float32
BF16 = jnp.bfloat16

D_MODEL = 2048
N_META = 16
HEAD_DIM = 64
ATTN_WIDTH = 1024
ATTN_HEADS = 16
ATTN_KV_HEADS = 4
KV_WIDTH = ATTN_KV_HEADS * HEAD_DIM
BLK = 128
PAD_ROWS = BLK - N_META
HG_HEADS = 8
HG_DIM = 128
HG_WIDTH = HG_HEADS * HG_DIM
HG_PER_STEP = 4
D_FF = 5632
EPS = 1e-6
NEG_INF = -1e30
LOG2_E = 1.4426950408889634
KK_WIDTH = ATTN_KV_HEADS * 128
QKV_COLS = ATTN_WIDTH + 2 * KK_WIDTH
IN_COLS2 = QKV_COLS + 5 * HG_WIDTH

TM = 1024
TM_RES = 512
TM_FFN = 768
TN_IN = 1024
TF = 512
HALO = 8
VMEM_LIMIT = 56 * 1024 * 1024


def _dot(a, b):
    return jnp.dot(a, b, preferred_element_type=F32)


def _dot_nt(a, b):
    return lax.dot_general(a, b, (((1,), (1,)), ((), ())), preferred_element_type=F32)


class _Layout(NamedTuple):
    nb_s: int
    nb_p: int
    base_p: int
    n_blocks: int
    seqs: tuple


def _make_layout(bp, sp, bs, ss):
    nb_s, nb_p = ss // BLK + 1, sp // BLK + 1
    base_p = -(-(bs * nb_s) // nb_p) * nb_p
    used = base_p + bp * nb_p
    per_tile = int(np.lcm.reduce([TM // BLK, TM_RES // BLK, TM_FFN // BLK]))
    n_blocks = -(-used // per_tile) * per_tile
    seqs = tuple((b * nb_s, nb_s) for b in range(bs)) + tuple((base_p + b * nb_p, nb_p) for b in range(bp))
    return _Layout(nb_s, nb_p, base_p, n_blocks, seqs)


def _block_tables(lay):
    n = lay.n_blocks
    thr = np.full((n,), BLK, np.int32)
    prev = np.arange(n, dtype=np.int32)
    nxt = np.arange(n, dtype=np.int32)
    meta = np.arange(n, dtype=np.int32)
    flags = np.zeros((n,), np.int32)
    for first, cnt in lay.seqs:
        for i in range(cnt):
            blk = first + i
            thr[blk] = PAD_ROWS if i == 0 else 0
            prev[blk] = max(blk - 1, first)
            nxt[blk] = min(blk + 1, first + cnt - 1)
            meta[blk] = first
            flags[blk] = (1 if i >= 2 else 0) | (2 if i >= 1 else 0) | (4 if i + 1 <= cnt - 1 else 0)
    return thr, prev, nxt, meta, flags


def _inproj_kernel(x_ref, g_ref, w_ref, o_ref, xn_ref):
    @pl.when(pl.program_id(1) == 0)
    def _():
        x = x_ref[...]
        ms = jnp.mean(x * x, axis=-1, keepdims=True)
        xn_ref[...] = (x * lax.rsqrt(ms + EPS) * g_ref[...]).astype(BF16)

    o_ref[...] = _dot(xn_ref[...], w_ref[...]).astype(BF16)


def _inproj(x, gain, w):
    rows = x.shape[0]
    return pl.pallas_call(
        _inproj_kernel,
        grid=(rows // TM, IN_COLS2 // TN_IN),
        in_specs=[
            pl.BlockSpec((TM, D_MODEL), lambda i, j: (i, 0)),
            pl.BlockSpec((1, D_MODEL), lambda i, j: (0, 0)),
            pl.BlockSpec((D_MODEL, TN_IN), lambda i, j: (0, j)),
        ],
        out_specs=pl.BlockSpec((TM, TN_IN), lambda i, j: (i, j)),
        out_shape=jax.ShapeDtypeStruct((rows, IN_COLS2), BF16),
        scratch_shapes=[pltpu.VMEM((TM, D_MODEL), BF16)],
        compiler_params=pltpu.CompilerParams(
            dimension_semantics=("arbitrary", "arbitrary"), vmem_limit_bytes=VMEM_LIMIT),
        name="inproj",
    )(x, gain, w)


HALF = BLK // 2
N_KEYS = 3 * BLK
KEY_PAD = N_KEYS - (2 * BLK + HALF + N_META)


def _attn_bias_tables():
    slopes = (2.0 ** (-8.0 * np.arange(1, ATTN_HEADS + 1) / ATTN_HEADS)).astype(np.float32)
    full = np.arange(BLK)
    dists, oks, groups = [], [], []
    for i, prev_j, next_j in ((np.arange(HALF), full, full[:HALF]), (HALF + np.arange(HALF), full[HALF:], full)):
        i = i[:, None]
        d_prev = BLK + i - prev_j[None]
        d_cur = np.abs(i - full[None])
        d_next = BLK + next_j[None] - i
        tail = np.zeros((HALF, N_META + KEY_PAD), np.int64)
        dists.append(np.concatenate([d_prev, d_cur, d_next, tail], axis=1))
        ok_tail = np.concatenate([np.ones((HALF, N_META), bool), np.zeros((HALF, KEY_PAD), bool)], axis=1)
        oks.append(np.concatenate([d_prev <= BLK, d_cur <= BLK, d_next <= BLK, ok_tail], axis=1))
        groups.append(np.concatenate([np.full(len(prev_j), 0), np.full(BLK, 1), np.full(len(next_j), 2),
                                      np.full(N_META + KEY_PAD, 3)]))
    dist = np.concatenate(dists, axis=0).astype(np.float32)
    ok = np.concatenate(oks, axis=0)
    group = np.stack(groups, axis=0).repeat(HALF, axis=0)
    flags = np.arange(8)[:, None, None]
    group_ok = np.where(group[None] == 3, True, ((flags >> np.minimum(group[None], 2)) & 1) > 0)
    valid = jnp.asarray(group_ok & ok[None])
    base = -(jnp.asarray(slopes)[:, None, None] * jnp.asarray(dist)[None]) * LOG2_E
    return jnp.where(valid[:, None], base[None], NEG_INF).astype(F32)


def _attn_kernel(prev_t, next_t, meta_t, flag_t, q_ref, kp_ref, kc_ref, kn_ref, km_ref,
                 t_ref, sink_ref, gain_ref, o_ref, acc_ref, kk_ref, vv_ref):
    del prev_t, next_t, meta_t
    lo = lax.broadcasted_iota(jnp.int32, (N_KEYS, BLK), 1) < HEAD_DIM
    lo_q = lax.broadcasted_iota(jnp.int32, (HALF, BLK), 1) < HEAD_DIM
    zero = jnp.zeros((), BF16)
    pad = jnp.zeros((KEY_PAD, BLK), BF16)
    ones_lo = jnp.where(lo, 1.0, 0.0).astype(BF16)
    for kh in range(ATTN_KV_HEADS):
        for half in range(2):
            vv_ref[kh, half, :N_KEYS, BLK:] = ones_lo
            vv_ref[kh, half, N_KEYS:, BLK:] = 1.0 - ones_lo

    def stack_keys(kh):
        kcols = slice(kh * BLK, (kh + 1) * BLK)
        vcols = slice(KK_WIDTH + kh * BLK, KK_WIDTH + (kh + 1) * BLK)
        for half in range(2):
            def keys(c):
                if half == 0:
                    parts = [kp_ref[:, c], kc_ref[:, c], kn_ref[:HALF, c]]
                else:
                    parts = [kp_ref[HALF:, c], kc_ref[:, c], kn_ref[:, c]]
                return jnp.concatenate(parts + [km_ref[PAD_ROWS:, c], pad], axis=0)

            kcat, vcat = keys(kcols), keys(vcols)
            kk_ref[kh, half, :N_KEYS, :] = jnp.where(lo, kcat, zero)
            kk_ref[kh, half, N_KEYS:, :] = jnp.where(lo, zero, kcat)
            vv_ref[kh, half, :N_KEYS, :BLK] = jnp.where(lo, vcat, zero)
            vv_ref[kh, half, N_KEYS:, :BLK] = jnp.where(lo, zero, vcat)

    def units(kh):
        return [(kh, half) for half in range(2)]

    def scores(kh, half):
        rows = slice(half * HALF, (half + 1) * HALF)
        q2 = jnp.concatenate([q_ref[rows, (2 * kh + j) * BLK:(2 * kh + j + 1) * BLK] for j in range(2)], axis=0)
        return _dot_nt(q2, kk_ref[kh, half])

    def softmax(s, kh, half):
        rows = slice(half * HALF, (half + 1) * HALF)
        ps, sinks = [], []
        for j in range(2):
            h_e = 4 * kh + 2 * j
            sj = s[j * HALF:(j + 1) * HALF, :]
            s_e = sj[:, :N_KEYS] + t_ref[0, h_e, rows, :]
            s_o = sj[:, N_KEYS:] + t_ref[0, h_e + 1, rows, :]
            m_e = jnp.maximum(jnp.max(s_e, axis=-1, keepdims=True), sink_ref[h_e])
            m_o = jnp.maximum(jnp.max(s_o, axis=-1, keepdims=True), sink_ref[h_e + 1])
            ps.append(jnp.concatenate([jnp.exp2(s_e - m_e), jnp.exp2(s_o - m_o)], axis=1).astype(BF16))
            sinks.append(jnp.where(lo_q, jnp.exp2(sink_ref[h_e] - m_e), jnp.exp2(sink_ref[h_e + 1] - m_o)))
        return jnp.concatenate(ps, axis=0), sinks

    def weighted_values(p, sinks, kh, half):
        rows = slice(half * HALF, (half + 1) * HALF)
        o4 = _dot(p, vv_ref[kh, half])
        for j in range(2):
            oj = o4[j * HALF:(j + 1) * HALF, :]
            col = (2 * kh + j) * BLK
            acc_ref[rows, col:col + BLK] = oj[:, :BLK] / (oj[:, BLK:] + sinks[j])

    for kh in range(ATTN_KV_HEADS):
        stack_keys(kh)
    pending = None
    for kh in range(ATTN_KV_HEADS + 1):
        s_now = [scores(*u) for u in units(kh)] if kh < ATTN_KV_HEADS else None
        if pending is not None:
            probs = [softmax(s, *u) for s, u in zip(pending, units(kh - 1))]
            for (p, sinks), u in zip(probs, units(kh - 1)):
                weighted_values(p, sinks, *u)
        pending = s_now

    o = acc_ref[...]
    ms = jnp.mean(o * o, axis=-1, keepdims=True)
    o_ref[...] = (o * lax.rsqrt(ms + EPS) * gain_ref[...]).astype(BF16)


def _attention(qkv, tables, bias, sink, gain):
    _, prev, nxt, meta, flags = tables
    n_blocks = qkv.shape[0] // BLK
    kv_spec = lambda f: pl.BlockSpec((BLK, 2 * KK_WIDTH), f)
    grid_spec = pltpu.PrefetchScalarGridSpec(
        num_scalar_prefetch=4,
        grid=(n_blocks,),
        in_specs=[
            pl.BlockSpec((BLK, ATTN_WIDTH), lambda i, p, n, m, f: (i, 0)),
            kv_spec(lambda i, p, n, m, f: (p[i], 1)),
            kv_spec(lambda i, p, n, m, f: (i, 1)),
            kv_spec(lambda i, p, n, m, f: (n[i], 1)),
            kv_spec(lambda i, p, n, m, f: (m[i], 1)),
            pl.BlockSpec((1, ATTN_HEADS, BLK, N_KEYS), lambda i, p, n, m, f: (f[i], 0, 0, 0)),
            pl.BlockSpec(memory_space=pltpu.SMEM),
            pl.BlockSpec((1, ATTN_WIDTH), lambda i, p, n, m, f: (0, 0)),
        ],
        out_specs=pl.BlockSpec((BLK, ATTN_WIDTH), lambda i, p, n, m, f: (i, 0)),
        scratch_shapes=[
            pltpu.VMEM((BLK, ATTN_WIDTH), F32),
            pltpu.VMEM((ATTN_KV_HEADS, 2, 2 * N_KEYS, BLK), BF16),
            pltpu.VMEM((ATTN_KV_HEADS, 2, 2 * N_KEYS, 2 * BLK), BF16),
        ],
    )
    return pl.pallas_call(
        _attn_kernel,
        grid_spec=grid_spec,
        out_shape=jax.ShapeDtypeStruct((qkv.shape[0], ATTN_WIDTH), BF16),
        compiler_params=pltpu.CompilerParams(
            dimension_semantics=("arbitrary",), vmem_limit_bytes=VMEM_LIMIT),
        name="attention",
    )(jnp.asarray(prev), jnp.asarray(nxt), jnp.asarray(meta), jnp.asarray(flags),
      qkv, qkv, qkv, qkv, qkv, bias, sink, gain)


_LEVELS = (64, 32, 16, 8, 4, 2, 1)


def _hgrn_consts():
    t = np.arange(BLK)[:, None]
    s = np.arange(BLK)[None, :]
    lmat = (s <= t).astype(np.float32)
    umat = (s >= t).astype(np.float32)
    x = t ^ s
    lv = np.where(x == 0, -1, np.floor(np.log2(np.maximum(x, 1)))).astype(np.int32)
    return jnp.asarray(lmat, BF16), jnp.asarray(umat, BF16), jnp.asarray(lv)


def _level_reference(ac_ref, c, rev):
    off = c - 1 + (1 if rev else 0)
    if c == 4:
        pieces = [jnp.broadcast_to(ac_ref[bs + off:bs + off + 1, :], (8, HG_DIM)) for bs in range(0, BLK, 8)]
        return jnp.concatenate(pieces, axis=0)
    assert c == 2
    sub = lax.broadcasted_iota(jnp.int32, (8, HG_DIM), 0)
    pieces = []
    for bs in range(0, BLK, 8):
        r1 = jnp.broadcast_to(ac_ref[bs + off:bs + off + 1, :], (8, HG_DIM))
        r2 = jnp.broadcast_to(ac_ref[bs + 4 + off:bs + 4 + off + 1, :], (8, HG_DIM))
        pieces.append(jnp.where(sub < 4, r1, r2))
    return jnp.concatenate(pieces, axis=0)


def _upper_level_operands(q, k, ac, ac_ref, c, rev):
    zeros = jnp.zeros((c, HG_DIM), F32)
    q_parts, k_parts = [], []
    for bs in range(0, BLK, 2 * c):
        lo_rows, hi_rows = slice(bs, bs + c), slice(bs + c, bs + 2 * c)
        q_rows, k_rows = (lo_rows, hi_rows) if rev else (hi_rows, lo_rows)
        r = bs + c if rev else bs + c - 1
        ref = ac_ref[r:r + 1, :]
        qb = q[q_rows] * jnp.exp2(ac[q_rows] - ref)
        kb = k[k_rows] * jnp.exp2(ref - ac[k_rows])
        q_parts += [qb, zeros] if rev else [zeros, qb]
        k_parts += [zeros, kb] if rev else [kb, zeros]
    return jnp.concatenate(q_parts, axis=0).astype(BF16), jnp.concatenate(k_parts, axis=0).astype(BF16)


def _hgrn_chunk(qs, zs, vs, first_row, lbps, cmat, lv, ac_refs, st_refs, rev):
    n = len(qs)
    off = 3 if rev else 0
    row = lax.broadcasted_iota(jnp.int32, (BLK, HG_DIM), 0)
    valid = row >= first_row

    lfs, ks, acs = [], [], []
    for q, z, lbp, ac_ref in zip(qs, zs, lbps, ac_refs):
        log_lb, log_1m_lb, one_m_lb = lbp[off:off + 1, :], lbp[off + 1:off + 2, :], lbp[off + 2:off + 3, :]
        log_sig = jnp.minimum(z, 0.0) - jnp.log(1.0 + jnp.exp(-jnp.abs(z)))
        bv = log_1m_lb + log_sig
        lf = jnp.maximum(log_lb, bv) + jnp.log(1.0 + jnp.exp(-jnp.abs(log_lb - bv)))
        k = one_m_lb * jnp.exp(log_sig - z)
        lf = jnp.where(valid, lf * LOG2_E, 0.0)
        k = jnp.where(valid, k, 0.0)
        hi = lf.astype(BF16)
        lo = (lf - hi.astype(F32)).astype(BF16)
        ac2 = _dot(cmat, jnp.concatenate([hi, lo], axis=1))
        ac = ac2[:, :HG_DIM] + ac2[:, HG_DIM:]
        ac_ref[...] = ac
        lfs.append(lf)
        ks.append(k)
        acs.append(ac)

    ps = [None] * n
    for c in _LEVELS:
        q_rows = ((row & c) == 0) if rev else ((row & c) != 0)
        for i in range(n):
            q, k, ac = qs[i], ks[i], acs[i]
            if c >= 8:
                qt, kt = _upper_level_operands(q, k, ac, ac_refs[i], c, rev)
            else:
                if c == 1:
                    w = jnp.where(q_rows, jnp.exp2(lfs[i]), 1.0)
                else:
                    d = ac - _level_reference(ac_refs[i], c, rev)
                    w = jnp.exp2(jnp.where(q_rows, d, -d))
                qt = jnp.where(q_rows, q * w, 0.0).astype(BF16)
                kt = jnp.where(q_rows, 0.0, k * w).astype(BF16)
            pc = _dot_nt(qt, kt)
            ps[i] = pc if ps[i] is None else jnp.where(lv == int(np.log2(c)), pc, ps[i])

    outs = []
    for i in range(n):
        p = jnp.where(lv == -1, jnp.sum(qs[i] * ks[i], axis=-1, keepdims=True), ps[i])
        st = st_refs[i][...]
        outs.append(_dot(p.astype(BF16), vs[i]) + _dot_nt((qs[i] * jnp.exp2(acs[i])).astype(BF16), st.astype(BF16)))
    for i in range(n):
        edge = ac_refs[i][0:1, :] if rev else ac_refs[i][BLK - 1:BLK, :]
        ke = (ks[i] * jnp.exp2(edge - acs[i])).astype(BF16)
        st_refs[i][...] = st_refs[i][...] * jnp.exp2(edge) + _dot(vs[i].T, ke)
    return outs


def _hgrn_kernel(q_ref, zf_ref, zb_ref, v_ref, g_ref, lbp_ref, gn_ref, lmat_ref, umat_ref, lv_ref,
                 alias_ref, o_ref, st_ref, obuf_ref, ac_ref, *, nrb, tb):
    del alias_ref
    ph = pl.program_id(2)
    rb = pl.program_id(3)
    nch = tb // BLK
    lv = lv_ref[...]

    @pl.when(rb == 0)
    def _():
        st_ref[...] = jnp.zeros_like(st_ref)

    def sweep(rev):
        z_ref = zb_ref if rev else zf_ref
        cmat = umat_ref[...] if rev else lmat_ref[...]
        rb_seq = (nrb - 1 - rb) if rev else rb

        def body(ci, carry):
            cj = (nch - 1 - ci) if rev else ci
            r0 = pl.multiple_of(cj * BLK, BLK)
            rows = pl.ds(r0, BLK)
            first_row = jnp.where(rb_seq * nch + cj > 0, 0, PAD_ROWS)
            seq_rows = pl.ds(pl.multiple_of(rb_seq * tb + r0, BLK), BLK)
            heads = range(HG_PER_STEP)
            col = [slice(hh * HG_DIM, (hh + 1) * HG_DIM) for hh in heads]
            outs = _hgrn_chunk([q_ref[rows, c].astype(F32) for c in col],
                               [z_ref[rows, c].astype(F32) for c in col],
                               [v_ref[rows, c] for c in col], first_row,
                               [lbp_ref.at[:, c] for c in col], cmat, lv,
                               [ac_ref.at[hh] for hh in heads], [st_ref.at[hh] for hh in heads], rev)
            for cols, o in zip(col, outs):
                if rev:
                    obuf_ref[seq_rows, cols] = o
                else:
                    o = o + obuf_ref[seq_rows, cols]
                    ms = jnp.mean(o * o, axis=-1, keepdims=True)
                    y = o * lax.rsqrt(ms + EPS) * gn_ref[:, cols]
                    g = g_ref[rows, cols].astype(F32)
                    o_ref[rows, cols] = (y * (g / (1.0 + jnp.exp(-g)))).astype(BF16)
            return carry

        lax.fori_loop(0, nch, body, 0)

    @pl.when(ph == 0)
    def _():
        sweep(True)

    @pl.when(ph == 1)
    def _():
        sweep(False)


def _hgrn_group(hg, rec, lbp, gn, consts, *, base_blk, batch, nb_seq):
    kmax = nb_seq if nb_seq * BLK <= 4096 else max(k for k in range(1, 5) if nb_seq % k == 0)
    tb = kmax * BLK
    nrb = nb_seq // kmax
    assert (base_blk * BLK) % tb == 0
    base = base_blk * BLK // tb
    lmat, umat, lv = consts

    def rows_in(b, h, ph, rb):
        return base + b * nrb + jnp.where(ph == 0, nrb - 1 - rb, rb)

    def rows_fwd_only(b, h, ph, rb):
        return base + b * nrb + jnp.where(ph == 0, 0, rb)

    def rows_bwd_only(b, h, ph, rb):
        return base + b * nrb + jnp.where(ph == 0, nrb - 1 - rb, 0)

    width = HG_PER_STEP * HG_DIM
    groups = HG_HEADS // HG_PER_STEP
    col_base = QKV_COLS // width
    blk = lambda rows_fn, col0: pl.BlockSpec(
        (tb, width), lambda b, h, ph, rb: (rows_fn(b, h, ph, rb), col0 + h))
    const = lambda shape: pl.BlockSpec(shape, lambda b, h, ph, rb: (0, 0))
    seq_rows = nb_seq * BLK
    return pl.pallas_call(
        functools.partial(_hgrn_kernel, nrb=nrb, tb=tb),
        grid=(batch, groups, 2, nrb),
        in_specs=[
            blk(rows_in, col_base),
            blk(rows_fwd_only, col_base + groups),
            blk(rows_bwd_only, col_base + 2 * groups),
            blk(rows_in, col_base + 3 * groups),
            blk(rows_fwd_only, col_base + 4 * groups),
            pl.BlockSpec((8, width), lambda b, h, ph, rb: (0, h)),
            pl.BlockSpec((1, width), lambda b, h, ph, rb: (0, h)),
            const((BLK, BLK)), const((BLK, BLK)), const((BLK, BLK)),
            pl.BlockSpec(memory_space=pl.ANY),
        ],
        out_specs=blk(rows_fwd_only, 0),
        out_shape=jax.ShapeDtypeStruct(rec.shape, rec.dtype),
        input_output_aliases={10: 0},
        scratch_shapes=[
            pltpu.VMEM((HG_PER_STEP, HG_DIM, HG_DIM), F32),
            pltpu.VMEM((seq_rows, width), F32),
            pltpu.VMEM((HG_PER_STEP, BLK, HG_DIM), F32),
        ],
        compiler_params=pltpu.CompilerParams(
            dimension_semantics=("arbitrary",) * 4, vmem_limit_bytes=VMEM_LIMIT),
        name="hgrn",
    )(hg, hg, hg, hg, hg, lbp, gn, lmat, umat, lv, rec)


def _store_valid_rows(o_ref, y, thr_ref, tile, rows_per_tile):
    row = lax.broadcasted_iota(jnp.int32, (BLK, y.shape[1]), 0)
    for kb in range(rows_per_tile // BLK):
        t = thr_ref[tile * (rows_per_tile // BLK) + kb]
        o_ref[kb * BLK:(kb + 1) * BLK, :] = jnp.where(row >= t, y[kb * BLK:(kb + 1) * BLK, :], 0.0)


def _outproj_kernel(thr_ref, a_ref, r_ref, x_ref, wa_ref, wr_ref, o_ref):
    y = x_ref[...] + _dot(a_ref[...], wa_ref[...]) + _dot(r_ref[...], wr_ref[...])
    _store_valid_rows(o_ref, y, thr_ref, pl.program_id(0), TM_RES)


def _outproj(thr, attn, rec, x, wa, wr):
    rows = x.shape[0]
    grid_spec = pltpu.PrefetchScalarGridSpec(
        num_scalar_prefetch=1,
        grid=(rows // TM_RES,),
        in_specs=[
            pl.BlockSpec((TM_RES, ATTN_WIDTH), lambda i, t: (i, 0)),
            pl.BlockSpec((TM_RES, HG_WIDTH), lambda i, t: (i, 0)),
            pl.BlockSpec((TM_RES, D_MODEL), lambda i, t: (i, 0)),
            pl.BlockSpec((ATTN_WIDTH, D_MODEL), lambda i, t: (0, 0)),
            pl.BlockSpec((HG_WIDTH, D_MODEL), lambda i, t: (0, 0)),
        ],
        out_specs=pl.BlockSpec((TM_RES, D_MODEL), lambda i, t: (i, 0)),
    )
    return pl.pallas_call(
        _outproj_kernel,
        grid_spec=grid_spec,
        out_shape=jax.ShapeDtypeStruct((rows, D_MODEL), F32),
        compiler_params=pltpu.CompilerParams(
            dimension_semantics=("arbitrary",), vmem_limit_bytes=VMEM_LIMIT),
        name="outproj",
    )(thr, attn, rec, x, wa, wr)


def _ffn_kernel(thr_ref, x_ref, xp_ref, xn_ref, g_ref, wa_ref, wb_ref, cwa_ref, cwb_ref, cba_ref, cbb_ref,
                wd_ref, gf_ref, o_ref, xs_ref, *, final):
    i = pl.program_id(0)
    f = pl.program_id(1)
    tm = x_ref.shape[0]
    ext = tm + 2 * HALO

    def norm(x):
        ms = jnp.mean(x * x, axis=-1, keepdims=True)
        return (x * lax.rsqrt(ms + EPS) * g_ref[...]).astype(BF16)

    @pl.when(f == 0)
    def _():
        xs_ref[HALO:HALO + tm, :] = norm(x_ref[...])
        xs_ref[0:HALO, :] = norm(xp_ref[...])
        nxt = jnp.where(i == pl.num_programs(0) - 1, 0.0, xn_ref[...])
        xs_ref[HALO + tm:ext, :] = norm(nxt)
        o_ref[...] = jnp.zeros_like(o_ref)

    xs = xs_ref[...]

    def conv(w_ref, cw_ref, cb_ref):
        u = _dot(xs, w_ref[...])
        u_prev = pltpu.roll(u, 1, 0)[HALO:HALO + tm, :]
        u_next = pltpu.roll(u, ext - 1, 0)[HALO:HALO + tm, :]
        cw = cw_ref[...]
        return cw[0:1, :] * u_prev + cw[1:2, :] * u[HALO:HALO + tm, :] + cw[2:3, :] * u_next + cb_ref[...]

    ca = conv(wa_ref, cwa_ref, cba_ref)
    cb = conv(wb_ref, cwb_ref, cbb_ref)
    act = (ca / (1.0 + jnp.exp(-ca))) * cb
    o_ref[...] += _dot(act.astype(BF16), wd_ref[...])

    @pl.when(f == pl.num_programs(1) - 1)
    def _():
        y = x_ref[...] + o_ref[...]
        if final:
            ms = jnp.mean(y * y, axis=-1, keepdims=True)
            y = y * lax.rsqrt(ms + EPS) * gf_ref[...]
        _store_valid_rows(o_ref, y, thr_ref, i, tm)


def _ffn(thr, x, gain, w_up, conv_w, conv_b, w_down, final_gain, *, final):
    rows = x.shape[0]
    nf = D_FF // TF
    per = TM_FFN // HALO
    last_halo = rows // HALO - 1
    grid_spec = pltpu.PrefetchScalarGridSpec(
        num_scalar_prefetch=1,
        grid=(rows // TM_FFN, nf),
        in_specs=[
            pl.BlockSpec((TM_FFN, D_MODEL), lambda i, f, t: (i, 0)),
            pl.BlockSpec((HALO, D_MODEL), lambda i, f, t: (jnp.maximum(i * per - 1, 0), 0)),
            pl.BlockSpec((HALO, D_MODEL), lambda i, f, t: (jnp.minimum((i + 1) * per, last_halo), 0)),
            pl.BlockSpec((1, D_MODEL), lambda i, f, t: (0, 0)),
            pl.BlockSpec((D_MODEL, TF), lambda i, f, t: (0, f)),
            pl.BlockSpec((D_MODEL, TF), lambda i, f, t: (0, nf + f)),
            pl.BlockSpec((3, TF), lambda i, f, t: (0, f)),
            pl.BlockSpec((3, TF), lambda i, f, t: (0, nf + f)),
            pl.BlockSpec((1, TF), lambda i, f, t: (0, f)),
            pl.BlockSpec((1, TF), lambda i, f, t: (0, nf + f)),
            pl.BlockSpec((TF, D_MODEL), lambda i, f, t: (f, 0)),
            pl.BlockSpec((1, D_MODEL), lambda i, f, t: (0, 0)),
        ],
        out_specs=pl.BlockSpec((TM_FFN, D_MODEL), lambda i, f, t: (i, 0)),
        scratch_shapes=[pltpu.VMEM((TM_FFN + 2 * HALO, D_MODEL), BF16)],
    )
    return pl.pallas_call(
        functools.partial(_ffn_kernel, final=final),
        grid_spec=grid_spec,
        out_shape=jax.ShapeDtypeStruct((rows, D_MODEL), F32),
        compiler_params=pltpu.CompilerParams(
            dimension_semantics=("arbitrary", "arbitrary"), vmem_limit_bytes=VMEM_LIMIT),
        name="ffn",
    )(thr, x, x, x, gain, w_up, w_up, conv_w, conv_w, conv_b, conv_b, w_down, final_gain)


def _pack_sequences(x, meta_tokens):
    b = x.shape[0]
    pad = jnp.zeros((b, PAD_ROWS, D_MODEL), x.dtype)
    meta = jnp.broadcast_to(meta_tokens.astype(x.dtype)[None], (b, N_META, D_MODEL))
    return jnp.concatenate([pad, meta, x], axis=1).reshape(-1, D_MODEL)


def _unpack_sequences(h, first_blk, batch, nb_seq):
    n_blocks = h.shape[0] // BLK
    if batch == 1:
        return h[(first_blk + 1) * BLK:(first_blk + nb_seq) * BLK][None]
    if first_blk % nb_seq == 0 and n_blocks % nb_seq == 0:
        slots = h.reshape(n_blocks // nb_seq, nb_seq, BLK, D_MODEL)
        first = first_blk // nb_seq
        return slots[first:first + batch, 1:].reshape(batch, (nb_seq - 1) * BLK, D_MODEL)
    rows = h[first_blk * BLK:(first_blk + batch * nb_seq) * BLK]
    return rows.reshape(batch, nb_seq * BLK, D_MODEL)[:, BLK:]


def _in_weights(w):
    q = w[:, :ATTN_WIDTH] * (HEAD_DIM ** -0.5 * LOG2_E)
    k = w[:, ATTN_WIDTH:ATTN_WIDTH + KV_WIDTH].reshape(D_MODEL, ATTN_KV_HEADS, 1, HEAD_DIM)
    v = w[:, ATTN_WIDTH + KV_WIDTH:ATTN_WIDTH + 2 * KV_WIDTH].reshape(D_MODEL, ATTN_KV_HEADS, 1, HEAD_DIM)
    kk = jnp.broadcast_to(k, (D_MODEL, ATTN_KV_HEADS, 2, HEAD_DIM)).reshape(D_MODEL, KK_WIDTH)
    vv = jnp.broadcast_to(v, (D_MODEL, ATTN_KV_HEADS, 2, HEAD_DIM)).reshape(D_MODEL, KK_WIDTH)
    hg = w[:, ATTN_WIDTH + 2 * KV_WIDTH:]
    return jnp.concatenate([q, kk, vv, hg], axis=1).astype(BF16)


def _lower_bound_params(lb_param, layer):
    c = jnp.cumsum(jax.nn.softmax(lb_param.astype(F32), axis=1), axis=1)
    lb = (c - c[:, :1])[:, layer]
    rows = [jnp.log(lb[0]), jnp.log1p(-lb[0]), 1.0 - lb[0], jnp.log(lb[1]), jnp.log1p(-lb[1]), 1.0 - lb[1]]
    rows += [jnp.zeros_like(lb[0])] * 2
    return jnp.stack(rows, axis=0)


def kernel(x_prompt, x_sample, meta_tokens, mix_norm, w_in, attn_sink, attn_out_norm, hgrn_lower_bounds,
           hgrn_out_norm, w_out, ffn_norm, w_up, conv_w, conv_b, w_down, final_norm):
    bp, sp, _ = x_prompt.shape
    bs, ss, _ = x_sample.shape
    depth = w_in.shape[0]
    lay = _make_layout(bp, sp, bs, ss)
    tables = _block_tables(lay)
    thr = jnp.asarray(tables[0])
    bias = _attn_bias_tables()
    consts = _hgrn_consts()

    rows_s = bs * lay.nb_s * BLK
    gap = lay.base_p * BLK - rows_s
    tail = lay.n_blocks * BLK - (lay.base_p + bp * lay.nb_p) * BLK
    parts = [_pack_sequences(x_sample, meta_tokens)]
    if gap:
        parts.append(jnp.zeros((gap, D_MODEL), F32))
    parts.append(_pack_sequences(x_prompt, meta_tokens))
    if tail:
        parts.append(jnp.zeros((tail, D_MODEL), F32))
    h = jnp.concatenate(parts, axis=0)
    rows = h.shape[0]

    row2 = lambda t: t.reshape(1, -1).astype(F32)
    for l in range(depth):
        proj = _inproj(h, row2(mix_norm[l]), _in_weights(w_in[l]))
        hg = proj
        attn = _attention(proj, tables, bias, attn_sink[l].astype(F32) * LOG2_E, row2(attn_out_norm[l]))
        lbp = _lower_bound_params(hgrn_lower_bounds, l)
        gn = row2(hgrn_out_norm[l])
        rec = jnp.zeros((rows, HG_WIDTH), BF16)
        rec = _hgrn_group(hg, rec, lbp, gn, consts, base_blk=0, batch=bs, nb_seq=lay.nb_s)
        rec = _hgrn_group(hg, rec, lbp, gn, consts, base_blk=lay.base_p, batch=bp, nb_seq=lay.nb_p)
        wo = w_out[l].astype(BF16)
        h = _outproj(thr, attn, rec, h, wo[:ATTN_WIDTH], wo[ATTN_WIDTH:])
        h = _ffn(thr, h, row2(ffn_norm[l]), w_up[l].astype(BF16), conv_w[l].astype(F32),
                 row2(conv_b[l]), w_down[l].astype(BF16), row2(final_norm), final=(l == depth - 1))

    return (_unpack_sequences(h, lay.base_p, bp, lay.nb_p), _unpack_sequences(h, 0, bs, lay.nb_s))
```

```python
import functools
from typing import NamedTuple

import numpy as np
import jax
import jax.numpy as jnp
from jax import lax
from jax.experimental import pallas as pl
from jax.experimental.pallas import tpu as pltpu

F32 = jnp.float32
BF16 = jnp.bfloat16

D_MODEL = 2048
N_META = 16
HEAD_DIM = 64
ATTN_WIDTH = 1024
ATTN_HEADS = 16
ATTN_KV_HEADS = 4
KV_WIDTH = ATTN_KV_HEADS * HEAD_DIM
BLK = 128
PAD_ROWS = BLK - N_META
HG_HEADS = 8
HG_DIM = 128
HG_WIDTH = HG_HEADS * HG_DIM
HG_PER_STEP = 4
D_FF = 5632
EPS = 1e-6
NEG_INF = -1e30
LOG2_E = 1.4426950408889634
KK_WIDTH = ATTN_KV_HEADS * 128
QKV_COLS = ATTN_WIDTH + 2 * KK_WIDTH
IN_COLS2 = QKV_COLS + 5 * HG_WIDTH

TM = 1024
TM_RES = 512
TM_FFN = 768
TN_IN = 1792
TF = 512
HALO = 8
VMEM_LIMIT = 56 * 1024 * 1024


def _dot(a, b):
    return jnp.dot(a, b, preferred_element_type=F32)


def _dot_nt(a, b):
    return lax.dot_general(a, b, (((1,), (1,)), ((), ())), preferred_element_type=F32)


class _Layout(NamedTuple):
    nb_s: int
    nb_p: int
    base_p: int
    n_blocks: int
    seqs: tuple


def _make_layout(bp, sp, bs, ss):
    nb_s, nb_p = ss // BLK + 1, sp // BLK + 1
    base_p = -(-(bs * nb_s) // nb_p) * nb_p
    used = base_p + bp * nb_p
    per_tile = int(np.lcm.reduce([TM // BLK, TM_RES // BLK, TM_FFN // BLK]))
    n_blocks = -(-used // per_tile) * per_tile
    seqs = tuple((b * nb_s, nb_s) for b in range(bs)) + tuple((base_p + b * nb_p, nb_p) for b in range(bp))
    return _Layout(nb_s, nb_p, base_p, n_blocks, seqs)


def _block_tables(lay):
    n = lay.n_blocks
    thr = np.full((n,), BLK, np.int32)
    prev = np.arange(n, dtype=np.int32)
    nxt = np.arange(n, dtype=np.int32)
    meta = np.arange(n, dtype=np.int32)
    flags = np.zeros((n,), np.int32)
    for first, cnt in lay.seqs:
        for i in range(cnt):
            blk = first + i
            thr[blk] = PAD_ROWS if i == 0 else 0
            prev[blk] = max(blk - 1, first)
            nxt[blk] = min(blk + 1, first + cnt - 1)
            meta[blk] = first
            flags[blk] = (1 if i >= 2 else 0) | (2 if i >= 1 else 0) | (4 if i + 1 <= cnt - 1 else 0)
    return thr, prev, nxt, meta, flags


def _inproj_kernel(x_ref, g_ref, w_ref, o_ref, xn_ref):
    @pl.when(pl.program_id(1) == 0)
    def _():
        x = x_ref[...]
        ms = jnp.mean(x * x, axis=-1, keepdims=True)
        xn_ref[...] = (x * lax.rsqrt(ms + EPS) * g_ref[...]).astype(BF16)

    o_ref[...] = _dot(xn_ref[...], w_ref[...]).astype(BF16)


def _inproj(x, gain, w):
    rows = x.shape[0]
    return pl.pallas_call(
        _inproj_kernel,
        grid=(rows // TM, IN_COLS2 // TN_IN),
        in_specs=[
            pl.BlockSpec((TM, D_MODEL), lambda i, j: (i, 0)),
            pl.BlockSpec((1, D_MODEL), lambda i, j: (0, 0)),
            pl.BlockSpec((D_MODEL, TN_IN), lambda i, j: (0, j)),
        ],
        out_specs=pl.BlockSpec((TM, TN_IN), lambda i, j: (i, j)),
        out_shape=jax.ShapeDtypeStruct((rows, IN_COLS2), BF16),
        scratch_shapes=[pltpu.VMEM((TM, D_MODEL), BF16)],
        compiler_params=pltpu.CompilerParams(
            dimension_semantics=("arbitrary", "arbitrary"), vmem_limit_bytes=VMEM_LIMIT),
        name="inproj",
    )(x, gain, w)


HALF = BLK // 2
N_KEYS = 3 * BLK
KEY_PAD = N_KEYS - (2 * BLK + HALF + N_META)


def _attn_bias_tables():
    slopes = (2.0 ** (-8.0 * np.arange(1, ATTN_HEADS + 1) / ATTN_HEADS)).astype(np.float32)
    full = np.arange(BLK)
    dists, oks, groups = [], [], []
    for i, prev_j, next_j in ((np.arange(HALF), full, full[:HALF]), (HALF + np.arange(HALF), full[HALF:], full)):
        i = i[:, None]
        d_prev = BLK + i - prev_j[None]
        d_cur = np.abs(i - full[None])
        d_next = BLK + next_j[None] - i
        tail = np.zeros((HALF, N_META + KEY_PAD), np.int64)
        dists.append(np.concatenate([d_prev, d_cur, d_next, tail], axis=1))
        ok_tail = np.concatenate([np.ones((HALF, N_META), bool), np.zeros((HALF, KEY_PAD), bool)], axis=1)
        oks.append(np.concatenate([d_prev <= BLK, d_cur <= BLK, d_next <= BLK, ok_tail], axis=1))
        groups.append(np.concatenate([np.full(len(prev_j), 0), np.full(BLK, 1), np.full(len(next_j), 2),
                                      np.full(N_META + KEY_PAD, 3)]))
    dist = np.concatenate(dists, axis=0).astype(np.float32)
    ok = np.concatenate(oks, axis=0)
    group = np.stack(groups, axis=0).repeat(HALF, axis=0)
    flags = np.arange(8)[:, None, None]
    group_ok = np.where(group[None] == 3, True, ((flags >> np.minimum(group[None], 2)) & 1) > 0)
    valid = jnp.asarray(group_ok & ok[None])
    base = -(jnp.asarray(slopes)[:, None, None] * jnp.asarray(dist)[None]) * LOG2_E
    return jnp.where(valid[:, None], base[None], NEG_INF).astype(F32)


def _attn_kernel(prev_t, next_t, meta_t, flag_t, q_ref, kp_ref, kc_ref, kn_ref, km_ref,
                 t_ref, sink_ref, gain_ref, o_ref, acc_ref, kk_ref, vv_ref):
    del prev_t, next_t, meta_t
    lo = lax.broadcasted_iota(jnp.int32, (N_KEYS, BLK), 1) < HEAD_DIM
    lo_q = lax.broadcasted_iota(jnp.int32, (HALF, BLK), 1) < HEAD_DIM
    zero = jnp.zeros((), BF16)
    pad = jnp.zeros((KEY_PAD, BLK), BF16)
    ones_lo = jnp.where(lo, 1.0, 0.0).astype(BF16)
    for kh in range(ATTN_KV_HEADS):
        for half in range(2):
            vv_ref[kh, half, :N_KEYS, BLK:] = ones_lo
            vv_ref[kh, half, N_KEYS:, BLK:] = 1.0 - ones_lo

    def stack_keys(kh):
        kcols = slice(kh * BLK, (kh + 1) * BLK)
        vcols = slice(KK_WIDTH + kh * BLK, KK_WIDTH + (kh + 1) * BLK)
        for half in range(2):
            def keys(c):
                if half == 0:
                    parts = [kp_ref[:, c], kc_ref[:, c], kn_ref[:HALF, c]]
                else:
                    parts = [kp_ref[HALF:, c], kc_ref[:, c], kn_ref[:, c]]
                return jnp.concatenate(parts + [km_ref[PAD_ROWS:, c], pad], axis=0)

            kcat, vcat = keys(kcols), keys(vcols)
            kk_ref[kh, half, :N_KEYS, :] = jnp.where(lo, kcat, zero)
            kk_ref[kh, half, N_KEYS:, :] = jnp.where(lo, zero, kcat)
            vv_ref[kh, half, :N_KEYS, :BLK] = jnp.where(lo, vcat, zero)
            vv_ref[kh, half, N_KEYS:, :BLK] = jnp.where(lo, zero, vcat)

    def units(kh):
        return [(kh, half) for half in range(2)]

    def scores(kh, half):
        rows = slice(half * HALF, (half + 1) * HALF)
        q2 = jnp.concatenate([q_ref[rows, (2 * kh + j) * BLK:(2 * kh + j + 1) * BLK] for j in range(2)], axis=0)
        return _dot_nt(q2, kk_ref[kh, half])

    def softmax(s, kh, half):
        rows = slice(half * HALF, (half + 1) * HALF)
        ps, sinks = [], []
        for j in range(2):
            h_e = 4 * kh + 2 * j
            sj = s[j * HALF:(j + 1) * HALF, :]
            s_e = sj[:, :N_KEYS] + t_ref[0, h_e, rows, :]
            s_o = sj[:, N_KEYS:] + t_ref[0, h_e + 1, rows, :]
            m_e = jnp.maximum(jnp.max(s_e, axis=-1, keepdims=True), sink_ref[h_e])
            m_o = jnp.maximum(jnp.max(s_o, axis=-1, keepdims=True), sink_ref[h_e + 1])
            ps.append(jnp.concatenate([jnp.exp2(s_e - m_e), jnp.exp2(s_o - m_o)], axis=1).astype(BF16))
            sinks.append(jnp.where(lo_q, jnp.exp2(sink_ref[h_e] - m_e), jnp.exp2(sink_ref[h_e + 1] - m_o)))
        return jnp.concatenate(ps, axis=0), sinks

    def weighted_values(p, sinks, kh, half):
        rows = slice(half * HALF, (half + 1) * HALF)
        o4 = _dot(p, vv_ref[kh, half])
        for j in range(2):
            oj = o4[j * HALF:(j + 1) * HALF, :]
            col = (2 * kh + j) * BLK
            acc_ref[rows, col:col + BLK] = oj[:, :BLK] / (oj[:, BLK:] + sinks[j])

    for kh in range(ATTN_KV_HEADS):
        stack_keys(kh)
    pending = None
    for kh in range(ATTN_KV_HEADS + 1):
        s_now = [scores(*u) for u in units(kh)] if kh < ATTN_KV_HEADS else None
        if pending is not None:
            probs = [softmax(s, *u) for s, u in zip(pending, units(kh - 1))]
            for (p, sinks), u in zip(probs, units(kh - 1)):
                weighted_values(p, sinks, *u)
        pending = s_now

    o = acc_ref[...]
    ms = jnp.mean(o * o, axis=-1, keepdims=True)
    o_ref[...] = (o * lax.rsqrt(ms + EPS) * gain_ref[...]).astype(BF16)


def _attention(qkv, tables, bias, sink, gain):
    _, prev, nxt, meta, flags = tables
    n_blocks = qkv.shape[0] // BLK
    kv_spec = lambda f: pl.BlockSpec((BLK, 2 * KK_WIDTH), f)
    grid_spec = pltpu.PrefetchScalarGridSpec(
        num_scalar_prefetch=4,
        grid=(n_blocks,),
        in_specs=[
            pl.BlockSpec((BLK, ATTN_WIDTH), lambda i, p, n, m, f: (i, 0)),
            kv_spec(lambda i, p, n, m, f: (p[i], 1)),
            kv_spec(lambda i, p, n, m, f: (i, 1)),
            kv_spec(lambda i, p, n, m, f: (n[i], 1)),
            kv_spec(lambda i, p, n, m, f: (m[i], 1)),
            pl.BlockSpec((1, ATTN_HEADS, BLK, N_KEYS), lambda i, p, n, m, f: (f[i], 0, 0, 0)),
            pl.BlockSpec(memory_space=pltpu.SMEM),
            pl.BlockSpec((1, ATTN_WIDTH), lambda i, p, n, m, f: (0, 0)),
        ],
        out_specs=pl.BlockSpec((BLK, ATTN_WIDTH), lambda i, p, n, m, f: (i, 0)),
        scratch_shapes=[
            pltpu.VMEM((BLK, ATTN_WIDTH), F32),
            pltpu.VMEM((ATTN_KV_HEADS, 2, 2 * N_KEYS, BLK), BF16),
            pltpu.VMEM((ATTN_KV_HEADS, 2, 2 * N_KEYS, 2 * BLK), BF16),
        ],
    )
    return pl.pallas_call(
        _attn_kernel,
        grid_spec=grid_spec,
        out_shape=jax.ShapeDtypeStruct((qkv.shape[0], ATTN_WIDTH), BF16),
        compiler_params=pltpu.CompilerParams(
            dimension_semantics=("arbitrary",), vmem_limit_bytes=VMEM_LIMIT),
        name="attention",
    )(jnp.asarray(prev), jnp.asarray(nxt), jnp.asarray(meta), jnp.asarray(flags),
      qkv, qkv, qkv, qkv, qkv, bias, sink, gain)


_LEVELS = (64, 32, 16, 8, 4, 2, 1)


def _hgrn_consts():
    t = np.arange(BLK)[:, None]
    s = np.arange(BLK)[None, :]
    lmat = (s <= t).astype(np.float32)
    umat = (s >= t).astype(np.float32)
    x = t ^ s
    lv = np.where(x == 0, -1, np.floor(np.log2(np.maximum(x, 1)))).astype(np.int32)
    return jnp.asarray(lmat, BF16), jnp.asarray(umat, BF16), jnp.asarray(lv)


def _level_reference(ac_ref, c, rev):
    off = c - 1 + (1 if rev else 0)
    if c == 4:
        pieces = [jnp.broadcast_to(ac_ref[bs + off:bs + off + 1, :], (8, HG_DIM)) for bs in range(0, BLK, 8)]
        return jnp.concatenate(pieces, axis=0)
    assert c == 2
    sub = lax.broadcasted_iota(jnp.int32, (8, HG_DIM), 0)
    pieces = []
    for bs in range(0, BLK, 8):
        r1 = jnp.broadcast_to(ac_ref[bs + off:bs + off + 1, :], (8, HG_DIM))
        r2 = jnp.broadcast_to(ac_ref[bs + 4 + off:bs + 4 + off + 1, :], (8, HG_DIM))
        pieces.append(jnp.where(sub < 4, r1, r2))
    return jnp.concatenate(pieces, axis=0)


def _upper_level_operands(q, k, ac, ac_ref, c, rev):
    zeros = jnp.zeros((c, HG_DIM), F32)
    q_parts, k_parts = [], []
    for bs in range(0, BLK, 2 * c):
        lo_rows, hi_rows = slice(bs, bs + c), slice(bs + c, bs + 2 * c)
        q_rows, k_rows = (lo_rows, hi_rows) if rev else (hi_rows, lo_rows)
        r = bs + c if rev else bs + c - 1
        ref = ac_ref[r:r + 1, :]
        qb = q[q_rows] * jnp.exp2(ac[q_rows] - ref)
        kb = k[k_rows] * jnp.exp2(ref - ac[k_rows])
        q_parts += [qb, zeros] if rev else [zeros, qb]
        k_parts += [zeros, kb] if rev else [kb, zeros]
    return jnp.concatenate(q_parts, axis=0).astype(BF16), jnp.concatenate(k_parts, axis=0).astype(BF16)


def _hgrn_chunk(qs, zs, vs, lbps, cmat, lv, ac_refs, st_refs, rev):
    n = len(qs)
    off = 3 if rev else 0
    row = lax.broadcasted_iota(jnp.int32, (BLK, HG_DIM), 0)

    lfs, ks, acs = [], [], []
    for q, z, lbp, ac_ref in zip(qs, zs, lbps, ac_refs):
        log_lb, log_1m_lb, one_m_lb = lbp[off:off + 1, :], lbp[off + 1:off + 2, :], lbp[off + 2:off + 3, :]
        log_sig = jnp.minimum(z, 0.0) - jnp.log(1.0 + jnp.exp(-jnp.abs(z)))
        bv = log_1m_lb + log_sig
        lf = jnp.maximum(log_lb, bv) + jnp.log(1.0 + jnp.exp(-jnp.abs(log_lb - bv)))
        k = one_m_lb * jnp.exp(log_sig - z)
        lf = lf * LOG2_E
        hi = lf.astype(BF16)
        lo = (lf - hi.astype(F32)).astype(BF16)
        ac2 = _dot(cmat, jnp.concatenate([hi, lo], axis=1))
        ac = ac2[:, :HG_DIM] + ac2[:, HG_DIM:]
        ac_ref[...] = ac
        lfs.append(lf)
        ks.append(k)
        acs.append(ac)

    ps = [None] * n
    for c in _LEVELS:
        q_rows = ((row & c) == 0) if rev else ((row & c) != 0)
        sign = jnp.where(q_rows, 1.0, -1.0)
        in_level = lv == int(np.log2(c))
        for i in range(n):
            q, k, ac = qs[i], ks[i], acs[i]
            if c >= 8:
                qt, kt = _upper_level_operands(q, k, ac, ac_refs[i], c, rev)
            else:
                if c == 1:
                    w = jnp.where(q_rows, jnp.exp2(lfs[i]), 1.0)
                else:
                    w = jnp.exp2((ac - _level_reference(ac_refs[i], c, rev)) * sign)
                qt = jnp.where(q_rows, q * w, 0.0).astype(BF16)
                kt = jnp.where(q_rows, 0.0, k * w).astype(BF16)
            pc = _dot_nt(qt, kt)
            ps[i] = pc if ps[i] is None else jnp.where(in_level, pc, ps[i])

    outs = []
    for i in range(n):
        p = jnp.where(lv == -1, jnp.sum(qs[i] * ks[i], axis=-1, keepdims=True), ps[i])
        st = st_refs[i][...]
        outs.append(_dot(p.astype(BF16), vs[i]) + _dot_nt((qs[i] * jnp.exp2(acs[i])).astype(BF16), st.astype(BF16)))
    for i in range(n):
        edge = ac_refs[i][0:1, :] if rev else ac_refs[i][BLK - 1:BLK, :]
        ke = (ks[i] * jnp.exp2(edge - acs[i])).astype(BF16)
        st_refs[i][...] = st_refs[i][...] * jnp.exp2(edge) + _dot(vs[i].T, ke)
    return outs


def _hgrn_kernel(q_ref, zf_ref, zb_ref, v_ref, g_ref, lbp_ref, gn_ref, lmat_ref, umat_ref, lv_ref,
                 alias_ref, o_ref, st_ref, obuf_ref, ac_ref, *, nrb, tb):
    del alias_ref
    ph = pl.program_id(2)
    rb = pl.program_id(3)
    nch = tb // BLK
    lv = lv_ref[...]

    @pl.when(rb == 0)
    def _():
        st_ref[...] = jnp.zeros_like(st_ref)

    def sweep(rev):
        z_ref = zb_ref if rev else zf_ref
        cmat = umat_ref[...] if rev else lmat_ref[...]
        rb_seq = (nrb - 1 - rb) if rev else rb

        def chunk(ci, parity):
            cj = (nch - 1 - ci) if rev else ci
            r0 = pl.multiple_of(cj * BLK, BLK)
            rows = pl.ds(r0, BLK)
            seq_rows = pl.ds(pl.multiple_of(rb_seq * tb + r0, BLK), BLK)
            heads = range(HG_PER_STEP)
            col = [slice(hh * HG_DIM, (hh + 1) * HG_DIM) for hh in heads]
            outs = _hgrn_chunk([q_ref[rows, c].astype(F32) for c in col],
                               [z_ref[rows, c].astype(F32) for c in col],
                               [v_ref[rows, c] for c in col],
                               [lbp_ref.at[:, c] for c in col], cmat, lv,
                               [ac_ref.at[parity, hh] for hh in heads], [st_ref.at[hh] for hh in heads], rev)
            for cols, o in zip(col, outs):
                if rev:
                    obuf_ref[seq_rows, cols] = o
                else:
                    o = o + obuf_ref[seq_rows, cols]
                    ms = jnp.mean(o * o, axis=-1, keepdims=True)
                    y = o * lax.rsqrt(ms + EPS) * gn_ref[:, cols]
                    g = g_ref[rows, cols].astype(F32)
                    o_ref[rows, cols] = (y * (g / (1.0 + jnp.exp(-g)))).astype(BF16)

        def pair(pi, carry):
            chunk(2 * pi, 0)
            chunk(2 * pi + 1, 1)
            return carry

        lax.fori_loop(0, nch // 2, pair, 0)
        if nch % 2:
            chunk(nch - 1, 0)

    @pl.when(ph == 0)
    def _():
        sweep(True)

    @pl.when(ph == 1)
    def _():
        sweep(False)


def _hgrn_group(hg, rec, lbp, gn, consts, *, base_blk, batch, nb_seq):
    kmax = nb_seq if nb_seq * BLK <= 4096 else max(k for k in range(1, 5) if nb_seq % k == 0)
    tb = kmax * BLK
    nrb = nb_seq // kmax
    assert (base_blk * BLK) % tb == 0
    base = base_blk * BLK // tb
    lmat, umat, lv = consts

    def rows_in(b, h, ph, rb):
        return base + b * nrb + jnp.where(ph == 0, nrb - 1 - rb, rb)

    def rows_fwd_only(b, h, ph, rb):
        return base + b * nrb + jnp.where(ph == 0, 0, rb)

    def rows_bwd_only(b, h, ph, rb):
        return base + b * nrb + jnp.where(ph == 0, nrb - 1 - rb, 0)

    width = HG_PER_STEP * HG_DIM
    groups = HG_HEADS // HG_PER_STEP
    col_base = QKV_COLS // width
    blk = lambda rows_fn, col0: pl.BlockSpec(
        (tb, width), lambda b, h, ph, rb: (rows_fn(b, h, ph, rb), col0 + h))
    const = lambda shape: pl.BlockSpec(shape, lambda b, h, ph, rb: (0, 0))
    seq_rows = nb_seq * BLK
    return pl.pallas_call(
        functools.partial(_hgrn_kernel, nrb=nrb, tb=tb),
        grid=(batch, groups, 2, nrb),
        in_specs=[
            blk(rows_in, col_base),
            blk(rows_fwd_only, col_base + groups),
            blk(rows_bwd_only, col_base + 2 * groups),
            blk(rows_in, col_base + 3 * groups),
            blk(rows_fwd_only, col_base + 4 * groups),
            pl.BlockSpec((8, width), lambda b, h, ph, rb: (0, h)),
            pl.BlockSpec((1, width), lambda b, h, ph, rb: (0, h)),
            const((BLK, BLK)), const((BLK, BLK)), const((BLK, BLK)),
            pl.BlockSpec(memory_space=pl.ANY),
        ],
        out_specs=blk(rows_fwd_only, 0),
        out_shape=jax.ShapeDtypeStruct(rec.shape, rec.dtype),
        input_output_aliases={10: 0},
        scratch_shapes=[
            pltpu.VMEM((HG_PER_STEP, HG_DIM, HG_DIM), F32),
            pltpu.VMEM((seq_rows, width), F32),
            pltpu.VMEM((2, HG_PER_STEP, BLK, HG_DIM), F32),
        ],
        compiler_params=pltpu.CompilerParams(
            dimension_semantics=("arbitrary",) * 4, vmem_limit_bytes=VMEM_LIMIT),
        name="hgrn",
    )(hg, hg, hg, hg, hg, lbp, gn, lmat, umat, lv, rec)


def _store_valid_rows(o_ref, y, thr_ref, tile, rows_per_tile):
    row = lax.broadcasted_iota(jnp.int32, (BLK, y.shape[1]), 0)
    for kb in range(rows_per_tile // BLK):
        t = thr_ref[tile * (rows_per_tile // BLK) + kb]
        o_ref[kb * BLK:(kb + 1) * BLK, :] = jnp.where(row >= t, y[kb * BLK:(kb + 1) * BLK, :], 0.0)


def _outproj_kernel(thr_ref, a_ref, r_ref, x_ref, wa_ref, wr_ref, o_ref):
    y = x_ref[...] + _dot(a_ref[...], wa_ref[...]) + _dot(r_ref[...], wr_ref[...])
    _store_valid_rows(o_ref, y, thr_ref, pl.program_id(0), TM_RES)


def _outproj(thr, attn, rec, x, wa, wr):
    rows = x.shape[0]
    grid_spec = pltpu.PrefetchScalarGridSpec(
        num_scalar_prefetch=1,
        grid=(rows // TM_RES,),
        in_specs=[
            pl.BlockSpec((TM_RES, ATTN_WIDTH), lambda i, t: (i, 0)),
            pl.BlockSpec((TM_RES, HG_WIDTH), lambda i, t: (i, 0)),
            pl.BlockSpec((TM_RES, D_MODEL), lambda i, t: (i, 0)),
            pl.BlockSpec((ATTN_WIDTH, D_MODEL), lambda i, t: (0, 0)),
            pl.BlockSpec((HG_WIDTH, D_MODEL), lambda i, t: (0, 0)),
        ],
        out_specs=pl.BlockSpec((TM_RES, D_MODEL), lambda i, t: (i, 0)),
    )
    return pl.pallas_call(
        _outproj_kernel,
        grid_spec=grid_spec,
        out_shape=jax.ShapeDtypeStruct((rows, D_MODEL), F32),
        compiler_params=pltpu.CompilerParams(
            dimension_semantics=("arbitrary",), vmem_limit_bytes=VMEM_LIMIT),
        name="outproj",
    )(thr, attn, rec, x, wa, wr)


def _ffn_kernel(thr_ref, x_ref, xp_ref, xn_ref, g_ref, wa_ref, wb_ref, cwa_ref, cwb_ref, cba_ref, cbb_ref,
                wd_ref, gf_ref, o_ref, xs_ref, *, final):
    i = pl.program_id(0)
    f = pl.program_id(1)
    tm = x_ref.shape[0]
    ext = tm + 2 * HALO

    def norm(x):
        ms = jnp.mean(x * x, axis=-1, keepdims=True)
        return (x * lax.rsqrt(ms + EPS) * g_ref[...]).astype(BF16)

    @pl.when(f == 0)
    def _():
        xs_ref[HALO:HALO + tm, :] = norm(x_ref[...])
        xs_ref[0:HALO, :] = norm(xp_ref[...])
        nxt = jnp.where(i == pl.num_programs(0) - 1, 0.0, xn_ref[...])
        xs_ref[HALO + tm:ext, :] = norm(nxt)
        o_ref[...] = jnp.zeros_like(o_ref)

    xs = xs_ref[...]

    def conv(w_ref, cw_ref, cb_ref):
        u = _dot(xs, w_ref[...])
        u_prev = pltpu.roll(u, 1, 0)[HALO:HALO + tm, :]
        u_next = pltpu.roll(u, ext - 1, 0)[HALO:HALO + tm, :]
        cw = cw_ref[...]
        return cw[0:1, :] * u_prev + cw[1:2, :] * u[HALO:HALO + tm, :] + cw[2:3, :] * u_next + cb_ref[...]

    ca = conv(wa_ref, cwa_ref, cba_ref)
    cb = conv(wb_ref, cwb_ref, cbb_ref)
    act = (ca / (1.0 + jnp.exp(-ca))) * cb
    o_ref[...] += _dot(act.astype(BF16), wd_ref[...])

    @pl.when(f == pl.num_programs(1) - 1)
    def _():
        y = x_ref[...] + o_ref[...]
        if final:
            ms = jnp.mean(y * y, axis=-1, keepdims=True)
            y = y * lax.rsqrt(ms + EPS) * gf_ref[...]
        _store_valid_rows(o_ref, y, thr_ref, i, tm)


def _ffn(thr, x, gain, w_up, conv_w, conv_b, w_down, final_gain, *, final):
    rows = x.shape[0]
    nf = D_FF // TF
    per = TM_FFN // HALO
    last_halo = rows // HALO - 1
    grid_spec = pltpu.PrefetchScalarGridSpec(
        num_scalar_prefetch=1,
        grid=(rows // TM_FFN, nf),
        in_specs=[
            pl.BlockSpec((TM_FFN, D_MODEL), lambda i, f, t: (i, 0)),
            pl.BlockSpec((HALO, D_MODEL), lambda i, f, t: (jnp.maximum(i * per - 1, 0), 0)),
            pl.BlockSpec((HALO, D_MODEL), lambda i, f, t: (jnp.minimum((i + 1) * per, last_halo), 0)),
            pl.BlockSpec((1, D_MODEL), lambda i, f, t: (0, 0)),
            pl.BlockSpec((D_MODEL, TF), lambda i, f, t: (0, f)),
            pl.BlockSpec((D_MODEL, TF), lambda i, f, t: (0, nf + f)),
            pl.BlockSpec((3, TF), lambda i, f, t: (0, f)),
            pl.BlockSpec((3, TF), lambda i, f, t: (0, nf + f)),
            pl.BlockSpec((1, TF), lambda i, f, t: (0, f)),
            pl.BlockSpec((1, TF), lambda i, f, t: (0, nf + f)),
            pl.BlockSpec((TF, D_MODEL), lambda i, f, t: (f, 0)),
            pl.BlockSpec((1, D_MODEL), lambda i, f, t: (0, 0)),
        ],
        out_specs=pl.BlockSpec((TM_FFN, D_MODEL), lambda i, f, t: (i, 0)),
        scratch_shapes=[pltpu.VMEM((TM_FFN + 2 * HALO, D_MODEL), BF16)],
    )
    return pl.pallas_call(
        functools.partial(_ffn_kernel, final=final),
        grid_spec=grid_spec,
        out_shape=jax.ShapeDtypeStruct((rows, D_MODEL), F32),
        compiler_params=pltpu.CompilerParams(
            dimension_semantics=("arbitrary", "arbitrary"), vmem_limit_bytes=VMEM_LIMIT),
        name="ffn",
    )(thr, x, x, x, gain, w_up, w_up, conv_w, conv_w, conv_b, conv_b, w_down, final_gain)


def _pack_sequences(x, meta_tokens):
    b = x.shape[0]
    pad = jnp.zeros((b, PAD_ROWS, D_MODEL), x.dtype)
    meta = jnp.broadcast_to(meta_tokens.astype(x.dtype)[None], (b, N_META, D_MODEL))
    return jnp.concatenate([pad, meta, x], axis=1).reshape(-1, D_MODEL)


def _unpack_sequences(h, first_blk, batch, nb_seq):
    n_blocks = h.shape[0] // BLK
    if batch == 1:
        return h[(first_blk + 1) * BLK:(first_blk + nb_seq) * BLK][None]
    if first_blk % nb_seq == 0 and n_blocks % nb_seq == 0:
        slots = h.reshape(n_blocks // nb_seq, nb_seq, BLK, D_MODEL)
        first = first_blk // nb_seq
        return slots[first:first + batch, 1:].reshape(batch, (nb_seq - 1) * BLK, D_MODEL)
    rows = h[first_blk * BLK:(first_blk + batch * nb_seq) * BLK]
    return rows.reshape(batch, nb_seq * BLK, D_MODEL)[:, BLK:]


def _in_weights(w):
    q = w[:, :ATTN_WIDTH] * (HEAD_DIM ** -0.5 * LOG2_E)
    k = w[:, ATTN_WIDTH:ATTN_WIDTH + KV_WIDTH].reshape(D_MODEL, ATTN_KV_HEADS, 1, HEAD_DIM)
    v = w[:, ATTN_WIDTH + KV_WIDTH:ATTN_WIDTH + 2 * KV_WIDTH].reshape(D_MODEL, ATTN_KV_HEADS, 1, HEAD_DIM)
    kk = jnp.broadcast_to(k, (D_MODEL, ATTN_KV_HEADS, 2, HEAD_DIM)).reshape(D_MODEL, KK_WIDTH)
    vv = jnp.broadcast_to(v, (D_MODEL, ATTN_KV_HEADS, 2, HEAD_DIM)).reshape(D_MODEL, KK_WIDTH)
    hg = w[:, ATTN_WIDTH + 2 * KV_WIDTH:]
    return jnp.concatenate([q, kk, vv, hg], axis=1).astype(BF16)


def _lower_bound_params(lb_param, layer):
    c = jnp.cumsum(jax.nn.softmax(lb_param.astype(F32), axis=1), axis=1)
    lb = (c - c[:, :1])[:, layer]
    rows = [jnp.log(lb[0]), jnp.log1p(-lb[0]), 1.0 - lb[0], jnp.log(lb[1]), jnp.log1p(-lb[1]), 1.0 - lb[1]]
    rows += [jnp.zeros_like(lb[0])] * 2
    return jnp.stack(rows, axis=0)


def kernel(x_prompt, x_sample, meta_tokens, mix_norm, w_in, attn_sink, attn_out_norm, hgrn_lower_bounds,
           hgrn_out_norm, w_out, ffn_norm, w_up, conv_w, conv_b, w_down, final_norm):
    bp, sp, _ = x_prompt.shape
    bs, ss, _ = x_sample.shape
    depth = w_in.shape[0]
    lay = _make_layout(bp, sp, bs, ss)
    tables = _block_tables(lay)
    thr = jnp.asarray(tables[0])
    bias = _attn_bias_tables()
    consts = _hgrn_consts()

    rows_s = bs * lay.nb_s * BLK
    gap = lay.base_p * BLK - rows_s
    tail = lay.n_blocks * BLK - (lay.base_p + bp * lay.nb_p) * BLK
    parts = [_pack_sequences(x_sample, meta_tokens)]
    if gap:
        parts.append(jnp.zeros((gap, D_MODEL), F32))
    parts.append(_pack_sequences(x_prompt, meta_tokens))
    if tail:
        parts.append(jnp.zeros((tail, D_MODEL), F32))
    h = jnp.concatenate(parts, axis=0)
    rows = h.shape[0]

    row2 = lambda t: t.reshape(1, -1).astype(F32)
    for l in range(depth):
        proj = _inproj(h, row2(mix_norm[l]), _in_weights(w_in[l]))
        hg = proj
        attn = _attention(proj, tables, bias, attn_sink[l].astype(F32) * LOG2_E, row2(attn_out_norm[l]))
        lbp = _lower_bound_params(hgrn_lower_bounds, l)
        gn = row2(hgrn_out_norm[l])
        rec = jnp.zeros((rows, HG_WIDTH), BF16)
        rec = _hgrn_group(hg, rec, lbp, gn, consts, base_blk=0, batch=bs, nb_seq=lay.nb_s)
        rec = _hgrn_group(hg, rec, lbp, gn, consts, base_blk=lay.base_p, batch=bp, nb_seq=lay.nb_p)
        wo = w_out[l].astype(BF16)
        h = _outproj(thr, attn, rec, h, wo[:ATTN_WIDTH], wo[ATTN_WIDTH:])
        h = _ffn(thr, h, row2(ffn_norm[l]), w_up[l].astype(BF16), conv_w[l].astype(F32),
                 row2(conv_b[l]), w_down[l].astype(BF16), row2(final_norm), final=(l == depth - 1))

    return (_unpack_sequences(h, lay.base_p, bp, lay.nb_p), _unpack_sequences(h, 0, bs, lay.nb_s))
```

```python
import functools
from typing import NamedTuple

import numpy as np
import jax
import jax.numpy as jnp
from jax import lax
from jax.experimental import pallas as pl
from jax.experimental.pallas import tpu as pltpu

F32 = jnp.float32
BF16 = jnp.bfloat16

D_MODEL = 2048
N_META = 16
HEAD_DIM = 64
ATTN_WIDTH = 1024
ATTN_HEADS = 16
ATTN_KV_HEADS = 4
KV_WIDTH = ATTN_KV_HEADS * HEAD_DIM
BLK = 128
PAD_ROWS = BLK - N_META
HG_HEADS = 8
HG_DIM = 128
HG_WIDTH = HG_HEADS * HG_DIM
HG_PER_STEP = 4
HG_CHUNKS_PER_BODY = 3
HG_MAX_SEQ_ROWS = 4096
HG_MAX_BLOCKS_PER_STEP = 4
D_FF = 5632
EPS = 1e-6
NEG_INF = -1e30
LOG2_E = 1.4426950408889634
KK_WIDTH = ATTN_KV_HEADS * 128
QKV_COLS = ATTN_WIDTH + 2 * KK_WIDTH
PLAIN_COLS = QKV_COLS + 3 * HG_WIDTH

TM = 1024
TM_RES = 512
TM_FFN = 768
TN_IN = 1280
TF = 512
HALO = 8
VMEM_LIMIT = 56 * 1024 * 1024


def _dot(a, b):
    return jnp.dot(a, b, preferred_element_type=F32)


def _dot_nt(a, b):
    return lax.dot_general(a, b, (((1,), (1,)), ((), ())), preferred_element_type=F32)


class _Layout(NamedTuple):
    nb_s: int
    nb_p: int
    base_p: int
    n_blocks: int
    seqs: tuple


def _make_layout(bp, sp, bs, ss):
    nb_s, nb_p = ss // BLK + 1, sp // BLK + 1
    base_p = -(-(bs * nb_s) // nb_p) * nb_p
    used = base_p + bp * nb_p
    per_tile = int(np.lcm.reduce([TM // BLK, TM_RES // BLK, TM_FFN // BLK]))
    n_blocks = -(-used // per_tile) * per_tile
    seqs = tuple((b * nb_s, nb_s) for b in range(bs)) + tuple((base_p + b * nb_p, nb_p) for b in range(bp))
    return _Layout(nb_s, nb_p, base_p, n_blocks, seqs)


def _block_tables(lay):
    n = lay.n_blocks
    thr = np.full((n,), BLK, np.int32)
    prev = np.arange(n, dtype=np.int32)
    nxt = np.arange(n, dtype=np.int32)
    meta = np.arange(n, dtype=np.int32)
    flags = np.zeros((n,), np.int32)
    for first, cnt in lay.seqs:
        for i in range(cnt):
            blk = first + i
            thr[blk] = PAD_ROWS if i == 0 else 0
            prev[blk] = max(blk - 1, first)
            nxt[blk] = min(blk + 1, first + cnt - 1)
            meta[blk] = first
            flags[blk] = (1 if i >= 2 else 0) | (2 if i >= 1 else 0) | (4 if i + 1 <= cnt - 1 else 0)
    return thr, prev, nxt, meta, flags


def _inproj_kernel(x_ref, g_ref, wp_ref, wz_ref, lbp_ref, o_ref, lf_ref, k_ref, xn_ref):
    @pl.when(pl.program_id(1) == 0)
    def _():
        x = x_ref[...]
        ms = jnp.mean(x * x, axis=-1, keepdims=True)
        xn_ref[...] = (x * lax.rsqrt(ms + EPS) * g_ref[...]).astype(BF16)

    xn = xn_ref[...]
    z = _dot(xn, wz_ref[...])
    o_ref[...] = _dot(xn, wp_ref[...]).astype(BF16)
    log_lb, log_1m_lb, one_m_lb = lbp_ref[0:1, :], lbp_ref[1:2, :], lbp_ref[2:3, :]
    log_sig = jnp.minimum(z, 0.0) - jnp.log(1.0 + jnp.exp(-jnp.abs(z)))
    bv = log_1m_lb + log_sig
    lf = jnp.maximum(log_lb, bv) + jnp.log(1.0 + jnp.exp(-jnp.abs(log_lb - bv)))
    lf_ref[...] = (lf * LOG2_E).astype(BF16)
    k_ref[...] = (one_m_lb * jnp.exp(log_sig - z)).astype(BF16)


def _inproj(x, gain, w_plain, w_z, lbp):
    rows = x.shape[0]
    steps = PLAIN_COLS // TN_IN
    tz = 2 * HG_WIDTH // steps
    return pl.pallas_call(
        _inproj_kernel,
        grid=(rows // TM, steps),
        in_specs=[
            pl.BlockSpec((TM, D_MODEL), lambda i, j: (i, 0)),
            pl.BlockSpec((1, D_MODEL), lambda i, j: (0, 0)),
            pl.BlockSpec((D_MODEL, TN_IN), lambda i, j: (0, j)),
            pl.BlockSpec((D_MODEL, tz), lambda i, j: (0, j)),
            pl.BlockSpec((8, tz), lambda i, j: (0, j)),
        ],
        out_specs=[
            pl.BlockSpec((TM, TN_IN), lambda i, j: (i, j)),
            pl.BlockSpec((TM, tz), lambda i, j: (i, j)),
            pl.BlockSpec((TM, tz), lambda i, j: (i, j)),
        ],
        out_shape=[
            jax.ShapeDtypeStruct((rows, PLAIN_COLS), BF16),
            jax.ShapeDtypeStruct((rows, 2 * HG_WIDTH), BF16),
            jax.ShapeDtypeStruct((rows, 2 * HG_WIDTH), BF16),
        ],
        scratch_shapes=[pltpu.VMEM((TM, D_MODEL), BF16)],
        compiler_params=pltpu.CompilerParams(
            dimension_semantics=("arbitrary", "arbitrary"), vmem_limit_bytes=VMEM_LIMIT),
        name="inproj",
    )(x, gain, w_plain, w_z, lbp)


HALF = BLK // 2
N_KEYS = 3 * BLK
KEY_PAD = N_KEYS - (2 * BLK + HALF + N_META)


def _attn_bias_tables():
    slopes = (2.0 ** (-8.0 * np.arange(1, ATTN_HEADS + 1) / ATTN_HEADS)).astype(np.float32)
    full = np.arange(BLK)
    dists, oks, groups = [], [], []
    for i, prev_j, next_j in ((np.arange(HALF), full, full[:HALF]), (HALF + np.arange(HALF), full[HALF:], full)):
        i = i[:, None]
        d_prev = BLK + i - prev_j[None]
        d_cur = np.abs(i - full[None])
        d_next = BLK + next_j[None] - i
        tail = np.zeros((HALF, N_META + KEY_PAD), np.int64)
        dists.append(np.concatenate([d_prev, d_cur, d_next, tail], axis=1))
        ok_tail = np.concatenate([np.ones((HALF, N_META), bool), np.zeros((HALF, KEY_PAD), bool)], axis=1)
        oks.append(np.concatenate([d_prev <= BLK, d_cur <= BLK, d_next <= BLK, ok_tail], axis=1))
        groups.append(np.concatenate([np.full(len(prev_j), 0), np.full(BLK, 1), np.full(len(next_j), 2),
                                      np.full(N_META + KEY_PAD, 3)]))
    dist = np.concatenate(dists, axis=0).astype(np.float32)
    ok = np.concatenate(oks, axis=0)
    group = np.stack(groups, axis=0).repeat(HALF, axis=0)
    flags = np.arange(8)[:, None, None]
    group_ok = np.where(group[None] == 3, True, ((flags >> np.minimum(group[None], 2)) & 1) > 0)
    valid = jnp.asarray(group_ok & ok[None])
    base = -(jnp.asarray(slopes)[:, None, None] * jnp.asarray(dist)[None]) * LOG2_E
    return jnp.where(valid[:, None], base[None], NEG_INF).astype(F32)


def _attn_kernel(prev_t, next_t, meta_t, flag_t, q_ref, kp_ref, kc_ref, kn_ref, km_ref,
                 t_ref, sink_ref, gain_ref, o_ref, acc_ref, kk_ref, vv_ref):
    del prev_t, next_t, meta_t
    lo = lax.broadcasted_iota(jnp.int32, (N_KEYS, BLK), 1) < HEAD_DIM
    lo_q = lax.broadcasted_iota(jnp.int32, (HALF, BLK), 1) < HEAD_DIM
    zero = jnp.zeros((), BF16)
    pad = jnp.zeros((KEY_PAD, BLK), BF16)
    ones_lo = jnp.where(lo, 1.0, 0.0).astype(BF16)
    for kh in range(ATTN_KV_HEADS):
        for half in range(2):
            vv_ref[kh, half, :N_KEYS, BLK:] = ones_lo
            vv_ref[kh, half, N_KEYS:, BLK:] = 1.0 - ones_lo

    def stack_keys(kh):
        kcols = slice(kh * BLK, (kh + 1) * BLK)
        vcols = slice(KK_WIDTH + kh * BLK, KK_WIDTH + (kh + 1) * BLK)
        for half in range(2):
            def keys(c):
                if half == 0:
                    parts = [kp_ref[:, c], kc_ref[:, c], kn_ref[:HALF, c]]
                else:
                    parts = [kp_ref[HALF:, c], kc_ref[:, c], kn_ref[:, c]]
                return jnp.concatenate(parts + [km_ref[PAD_ROWS:, c], pad], axis=0)

            kcat, vcat = keys(kcols), keys(vcols)
            kk_ref[kh, half, :N_KEYS, :] = jnp.where(lo, kcat, zero)
            kk_ref[kh, half, N_KEYS:, :] = jnp.where(lo, zero, kcat)
            vv_ref[kh, half, :N_KEYS, :BLK] = jnp.where(lo, vcat, zero)
            vv_ref[kh, half, N_KEYS:, :BLK] = jnp.where(lo, zero, vcat)

    def units(kh):
        return [(kh, half) for half in range(2)]

    def scores(kh, half):
        rows = slice(half * HALF, (half + 1) * HALF)
        q2 = jnp.concatenate([q_ref[rows, (2 * kh + j) * BLK:(2 * kh + j + 1) * BLK] for j in range(2)], axis=0)
        return _dot_nt(q2, kk_ref[kh, half])

    def softmax(s, kh, half):
        rows = slice(half * HALF, (half + 1) * HALF)
        ps, sinks = [], []
        for j in range(2):
            h_e = 4 * kh + 2 * j
            sj = s[j * HALF:(j + 1) * HALF, :]
            s_e = sj[:, :N_KEYS] + t_ref[0, h_e, rows, :]
            s_o = sj[:, N_KEYS:] + t_ref[0, h_e + 1, rows, :]
            m_e = jnp.maximum(jnp.max(s_e, axis=-1, keepdims=True), sink_ref[h_e])
            m_o = jnp.maximum(jnp.max(s_o, axis=-1, keepdims=True), sink_ref[h_e + 1])
            ps.append(jnp.concatenate([jnp.exp2(s_e - m_e), jnp.exp2(s_o - m_o)], axis=1).astype(BF16))
            sinks.append(jnp.where(lo_q, jnp.exp2(sink_ref[h_e] - m_e), jnp.exp2(sink_ref[h_e + 1] - m_o)))
        return jnp.concatenate(ps, axis=0), sinks

    def weighted_values(p, sinks, kh, half):
        rows = slice(half * HALF, (half + 1) * HALF)
        o4 = _dot(p, vv_ref[kh, half])
        for j in range(2):
            oj = o4[j * HALF:(j + 1) * HALF, :]
            col = (2 * kh + j) * BLK
            acc_ref[rows, col:col + BLK] = oj[:, :BLK] / (oj[:, BLK:] + sinks[j])

    for kh in range(ATTN_KV_HEADS):
        stack_keys(kh)
    pending = None
    for kh in range(ATTN_KV_HEADS + 1):
        s_now = [scores(*u) for u in units(kh)] if kh < ATTN_KV_HEADS else None
        if pending is not None:
            probs = [softmax(s, *u) for s, u in zip(pending, units(kh - 1))]
            for (p, sinks), u in zip(probs, units(kh - 1)):
                weighted_values(p, sinks, *u)
        pending = s_now

    o = acc_ref[...]
    ms = jnp.mean(o * o, axis=-1, keepdims=True)
    o_ref[...] = (o * lax.rsqrt(ms + EPS) * gain_ref[...]).astype(BF16)


def _attention(qkv, tables, bias, sink, gain):
    _, prev, nxt, meta, flags = tables
    n_blocks = qkv.shape[0] // BLK
    kv_spec = lambda f: pl.BlockSpec((BLK, 2 * KK_WIDTH), f)
    grid_spec = pltpu.PrefetchScalarGridSpec(
        num_scalar_prefetch=4,
        grid=(n_blocks,),
        in_specs=[
            pl.BlockSpec((BLK, ATTN_WIDTH), lambda i, p, n, m, f: (i, 0)),
            kv_spec(lambda i, p, n, m, f: (p[i], 1)),
            kv_spec(lambda i, p, n, m, f: (i, 1)),
            kv_spec(lambda i, p, n, m, f: (n[i], 1)),
            kv_spec(lambda i, p, n, m, f: (m[i], 1)),
            pl.BlockSpec((1, ATTN_HEADS, BLK, N_KEYS), lambda i, p, n, m, f: (f[i], 0, 0, 0)),
            pl.BlockSpec(memory_space=pltpu.SMEM),
            pl.BlockSpec((1, ATTN_WIDTH), lambda i, p, n, m, f: (0, 0)),
        ],
        out_specs=pl.BlockSpec((BLK, ATTN_WIDTH), lambda i, p, n, m, f: (i, 0)),
        scratch_shapes=[
            pltpu.VMEM((BLK, ATTN_WIDTH), F32),
            pltpu.VMEM((ATTN_KV_HEADS, 2, 2 * N_KEYS, BLK), BF16),
            pltpu.VMEM((ATTN_KV_HEADS, 2, 2 * N_KEYS, 2 * BLK), BF16),
        ],
    )
    return pl.pallas_call(
        _attn_kernel,
        grid_spec=grid_spec,
        out_shape=jax.ShapeDtypeStruct((qkv.shape[0], ATTN_WIDTH), BF16),
        compiler_params=pltpu.CompilerParams(
            dimension_semantics=("arbitrary",), vmem_limit_bytes=VMEM_LIMIT),
        name="attention",
    )(jnp.asarray(prev), jnp.asarray(nxt), jnp.asarray(meta), jnp.asarray(flags),
      qkv, qkv, qkv, qkv, qkv, bias, sink, gain)


_LEVELS = (64, 32, 16, 8, 4, 2, 1)


def _hgrn_consts():
    t = np.arange(BLK)[:, None]
    s = np.arange(BLK)[None, :]
    lmat = (s <= t).astype(np.float32)
    umat = (s >= t).astype(np.float32)
    x = t ^ s
    lv = np.where(x == 0, -1, np.floor(np.log2(np.maximum(x, 1)))).astype(np.int32)
    return jnp.asarray(lmat, BF16), jnp.asarray(umat, BF16), jnp.asarray(lv)


def _level_reference(ac_ref, c, rev):
    off = c - 1 + (1 if rev else 0)
    if c == 4:
        pieces = [jnp.broadcast_to(ac_ref[bs + off:bs + off + 1, :], (8, HG_DIM)) for bs in range(0, BLK, 8)]
        return jnp.concatenate(pieces, axis=0)
    assert c == 2
    sub = lax.broadcasted_iota(jnp.int32, (8, HG_DIM), 0)
    pieces = []
    for bs in range(0, BLK, 8):
        r1 = jnp.broadcast_to(ac_ref[bs + off:bs + off + 1, :], (8, HG_DIM))
        r2 = jnp.broadcast_to(ac_ref[bs + 4 + off:bs + 4 + off + 1, :], (8, HG_DIM))
        pieces.append(jnp.where(sub < 4, r1, r2))
    return jnp.concatenate(pieces, axis=0)


def _upper_level_operands(q, k, ac, ac_ref, c, rev):
    zeros = jnp.zeros((c, HG_DIM), F32)
    q_parts, k_parts = [], []
    for bs in range(0, BLK, 2 * c):
        lo_rows, hi_rows = slice(bs, bs + c), slice(bs + c, bs + 2 * c)
        q_rows, k_rows = (lo_rows, hi_rows) if rev else (hi_rows, lo_rows)
        r = bs + c if rev else bs + c - 1
        ref = ac_ref[r:r + 1, :]
        qb = q[q_rows] * jnp.exp2(ac[q_rows] - ref)
        kb = k[k_rows] * jnp.exp2(ref - ac[k_rows])
        q_parts += [qb, zeros] if rev else [zeros, qb]
        k_parts += [zeros, kb] if rev else [kb, zeros]
    return jnp.concatenate(q_parts, axis=0).astype(BF16), jnp.concatenate(k_parts, axis=0).astype(BF16)


def _hgrn_chunk(qs, lf16s, ks, vs, cmat, lv, ac_refs, st_refs, rev):
    n = len(qs)
    row = lax.broadcasted_iota(jnp.int32, (BLK, HG_DIM), 0)

    lfs, acs = [], []
    for lf16, ac_ref in zip(lf16s, ac_refs):
        ac = _dot(cmat, lf16)
        ac_ref[...] = ac
        lfs.append(lf16.astype(F32))
        acs.append(ac)

    ps = [None] * n
    for c in _LEVELS:
        q_rows = ((row & c) == 0) if rev else ((row & c) != 0)
        sign = jnp.where(q_rows, 1.0, -1.0)
        in_level = lv == int(np.log2(c))
        for i in range(n):
            q, k, ac = qs[i], ks[i], acs[i]
            if c >= 8:
                qt, kt = _upper_level_operands(q, k, ac, ac_refs[i], c, rev)
            else:
                if c == 1:
                    w = jnp.where(q_rows, jnp.exp2(lfs[i]), 1.0)
                else:
                    w = jnp.exp2((ac - _level_reference(ac_refs[i], c, rev)) * sign)
                qt = jnp.where(q_rows, q * w, 0.0).astype(BF16)
                kt = jnp.where(q_rows, 0.0, k * w).astype(BF16)
            pc = _dot_nt(qt, kt).astype(BF16)
            ps[i] = pc if ps[i] is None else jnp.where(in_level, pc, ps[i])

    outs = []
    for i in range(n):
        diag = jnp.sum(qs[i] * ks[i], axis=-1, keepdims=True).astype(BF16)
        p = jnp.where(lv == -1, diag, ps[i])
        st = st_refs[i][...]
        outs.append(_dot(p, vs[i]) + _dot_nt((qs[i] * jnp.exp2(acs[i])).astype(BF16), st.astype(BF16)))
    for i in range(n):
        edge = ac_refs[i][0:1, :] if rev else ac_refs[i][BLK - 1:BLK, :]
        ke = (ks[i] * jnp.exp2(edge - acs[i])).astype(BF16)
        st_refs[i][...] = st_refs[i][...] * jnp.exp2(edge) + _dot(vs[i].T, ke)
    return outs


def _hgrn_kernel(q_ref, lff_ref, kf_ref, lfb_ref, kb_ref, v_ref, g_ref, gn_ref, lmat_ref, umat_ref, lv_ref,
                 alias_ref, o_ref, st_ref, obuf_ref, ac_ref, *, nrb, tb):
    del alias_ref
    ph = pl.program_id(2)
    rb = pl.program_id(3)
    nch = tb // BLK
    lv = lv_ref[...]

    @pl.when(rb == 0)
    def _():
        st_ref[...] = jnp.zeros_like(st_ref)

    def sweep(rev):
        lf_ref, k_ref = (lfb_ref, kb_ref) if rev else (lff_ref, kf_ref)
        cmat = umat_ref[...] if rev else lmat_ref[...]
        rb_seq = (nrb - 1 - rb) if rev else rb

        def chunk(ci, parity):
            cj = (nch - 1 - ci) if rev else ci
            r0 = pl.multiple_of(cj * BLK, BLK)
            rows = pl.ds(r0, BLK)
            seq_rows = pl.ds(pl.multiple_of(rb_seq * tb + r0, BLK), BLK)
            heads = range(HG_PER_STEP)
            col = [slice(hh * HG_DIM, (hh + 1) * HG_DIM) for hh in heads]
            outs = _hgrn_chunk([q_ref[rows, c].astype(F32) for c in col],
                               [lf_ref[rows, c] for c in col],
                               [k_ref[rows, c].astype(F32) for c in col],
                               [v_ref[rows, c] for c in col], cmat, lv,
                               [ac_ref.at[parity, hh] for hh in heads], [st_ref.at[hh] for hh in heads], rev)
            for cols, o in zip(col, outs):
                if rev:
                    obuf_ref[seq_rows, cols] = o
                else:
                    o = o + obuf_ref[seq_rows, cols]
                    ms = jnp.mean(o * o, axis=-1, keepdims=True)
                    y = o * lax.rsqrt(ms + EPS) * gn_ref[:, cols]
                    g = g_ref[rows, cols].astype(F32)
                    o_ref[rows, cols] = (y * (g / (1.0 + jnp.exp(-g)))).astype(BF16)

        def body(bi, carry):
            for u in range(HG_CHUNKS_PER_BODY):
                chunk(HG_CHUNKS_PER_BODY * bi + u, u)
            return carry

        n_bodies = nch // HG_CHUNKS_PER_BODY
        lax.fori_loop(0, n_bodies, body, 0)
        for u in range(nch % HG_CHUNKS_PER_BODY):
            chunk(n_bodies * HG_CHUNKS_PER_BODY + u, u)

    @pl.when(ph == 0)
    def _():
        sweep(True)

    @pl.when(ph == 1)
    def _():
        sweep(False)


def _hgrn_group(proj, lf, kg, rec, gn, consts, *, base_blk, batch, nb_seq):
    if nb_seq * BLK <= HG_MAX_SEQ_ROWS:
        kmax = nb_seq
    else:
        kmax = max(k for k in range(1, HG_MAX_BLOCKS_PER_STEP + 1) if nb_seq % k == 0)
    tb = kmax * BLK
    nrb = nb_seq // kmax
    assert (base_blk * BLK) % tb == 0
    base = base_blk * BLK // tb
    lmat, umat, lv = consts

    def rows_in(b, h, ph, rb):
        return base + b * nrb + jnp.where(ph == 0, nrb - 1 - rb, rb)

    def rows_fwd_only(b, h, ph, rb):
        return base + b * nrb + jnp.where(ph == 0, 0, rb)

    def rows_bwd_only(b, h, ph, rb):
        return base + b * nrb + jnp.where(ph == 0, nrb - 1 - rb, 0)

    width = HG_PER_STEP * HG_DIM
    groups = HG_HEADS // HG_PER_STEP
    col_base = QKV_COLS // width
    blk = lambda rows_fn, col0: pl.BlockSpec(
        (tb, width), lambda b, h, ph, rb: (rows_fn(b, h, ph, rb), col0 + h))
    const = lambda shape: pl.BlockSpec(shape, lambda b, h, ph, rb: (0, 0))
    seq_rows = nb_seq * BLK
    return pl.pallas_call(
        functools.partial(_hgrn_kernel, nrb=nrb, tb=tb),
        grid=(batch, groups, 2, nrb),
        in_specs=[
            blk(rows_in, col_base),
            blk(rows_fwd_only, 0),
            blk(rows_fwd_only, 0),
            blk(rows_bwd_only, groups),
            blk(rows_bwd_only, groups),
            blk(rows_in, col_base + groups),
            blk(rows_fwd_only, col_base + 2 * groups),
            pl.BlockSpec((1, width), lambda b, h, ph, rb: (0, h)),
            const((BLK, BLK)), const((BLK, BLK)), const((BLK, BLK)),
            pl.BlockSpec(memory_space=pl.ANY),
        ],
        out_specs=blk(rows_fwd_only, 0),
        out_shape=jax.ShapeDtypeStruct(rec.shape, rec.dtype),
        input_output_aliases={11: 0},
        scratch_shapes=[
            pltpu.VMEM((HG_PER_STEP, HG_DIM, HG_DIM), F32),
            pltpu.VMEM((seq_rows, width), F32),
            pltpu.VMEM((HG_CHUNKS_PER_BODY, HG_PER_STEP, BLK, HG_DIM), F32),
        ],
        compiler_params=pltpu.CompilerParams(
            dimension_semantics=("arbitrary",) * 4, vmem_limit_bytes=VMEM_LIMIT),
        name="hgrn",
    )(proj, lf, kg, lf, kg, proj, proj, gn, lmat, umat, lv, rec)


def _store_valid_rows(o_ref, y, thr_ref, tile, rows_per_tile):
    row = lax.broadcasted_iota(jnp.int32, (BLK, y.shape[1]), 0)
    for kb in range(rows_per_tile // BLK):
        t = thr_ref[tile * (rows_per_tile // BLK) + kb]
        o_ref[kb * BLK:(kb + 1) * BLK, :] = jnp.where(row >= t, y[kb * BLK:(kb + 1) * BLK, :], 0.0)


def _outproj_kernel(thr_ref, a_ref, r_ref, x_ref, wa_ref, wr_ref, o_ref):
    y = x_ref[...] + _dot(a_ref[...], wa_ref[...]) + _dot(r_ref[...], wr_ref[...])
    _store_valid_rows(o_ref, y, thr_ref, pl.program_id(0), TM_RES)


def _outproj(thr, attn, rec, x, wa, wr):
    rows = x.shape[0]
    grid_spec = pltpu.PrefetchScalarGridSpec(
        num_scalar_prefetch=1,
        grid=(rows // TM_RES,),
        in_specs=[
            pl.BlockSpec((TM_RES, ATTN_WIDTH), lambda i, t: (i, 0)),
            pl.BlockSpec((TM_RES, HG_WIDTH), lambda i, t: (i, 0)),
            pl.BlockSpec((TM_RES, D_MODEL), lambda i, t: (i, 0)),
            pl.BlockSpec((ATTN_WIDTH, D_MODEL), lambda i, t: (0, 0)),
            pl.BlockSpec((HG_WIDTH, D_MODEL), lambda i, t: (0, 0)),
        ],
        out_specs=pl.BlockSpec((TM_RES, D_MODEL), lambda i, t: (i, 0)),
    )
    return pl.pallas_call(
        _outproj_kernel,
        grid_spec=grid_spec,
        out_shape=jax.ShapeDtypeStruct((rows, D_MODEL), F32),
        compiler_params=pltpu.CompilerParams(
            dimension_semantics=("arbitrary",), vmem_limit_bytes=VMEM_LIMIT),
        name="outproj",
    )(thr, attn, rec, x, wa, wr)


def _ffn_kernel(thr_ref, x_ref, xp_ref, xn_ref, g_ref, wa_ref, wb_ref, cwa_ref, cwb_ref, cba_ref, cbb_ref,
                wd_ref, gf_ref, o_ref, xs_ref, *, final):
    i = pl.program_id(0)
    f = pl.program_id(1)
    tm = x_ref.shape[0]
    ext = tm + 2 * HALO

    def norm(x):
        ms = jnp.mean(x * x, axis=-1, keepdims=True)
        return (x * lax.rsqrt(ms + EPS) * g_ref[...]).astype(BF16)

    @pl.when(f == 0)
    def _():
        xs_ref[HALO:HALO + tm, :] = norm(x_ref[...])
        xs_ref[0:HALO, :] = norm(xp_ref[...])
        nxt = jnp.where(i == pl.num_programs(0) - 1, 0.0, xn_ref[...])
        xs_ref[HALO + tm:ext, :] = norm(nxt)
        o_ref[...] = jnp.zeros_like(o_ref)

    xs = xs_ref[...]

    def conv(w_ref, cw_ref, cb_ref):
        u = _dot(xs, w_ref[...])
        u_prev = pltpu.roll(u, 1, 0)[HALO:HALO + tm, :]
        u_next = pltpu.roll(u, ext - 1, 0)[HALO:HALO + tm, :]
        cw = cw_ref[...]
        return cw[0:1, :] * u_prev + cw[1:2, :] * u[HALO:HALO + tm, :] + cw[2:3, :] * u_next + cb_ref[...]

    ca = conv(wa_ref, cwa_ref, cba_ref)
    cb = conv(wb_ref, cwb_ref, cbb_ref)
    act = (ca / (1.0 + jnp.exp(-ca))) * cb
    o_ref[...] += _dot(act.astype(BF16), wd_ref[...])

    @pl.when(f == pl.num_programs(1) - 1)
    def _():
        y = x_ref[...] + o_ref[...]
        if final:
            ms = jnp.mean(y * y, axis=-1, keepdims=True)
            y = y * lax.rsqrt(ms + EPS) * gf_ref[...]
        _store_valid_rows(o_ref, y, thr_ref, i, tm)


def _ffn(thr, x, gain, w_up, conv_w, conv_b, w_down, final_gain, *, final):
    rows = x.shape[0]
    nf = D_FF // TF
    per = TM_FFN // HALO
    last_halo = rows // HALO - 1
    grid_spec = pltpu.PrefetchScalarGridSpec(
        num_scalar_prefetch=1,
        grid=(rows // TM_FFN, nf),
        in_specs=[
            pl.BlockSpec((TM_FFN, D_MODEL), lambda i, f, t: (i, 0)),
            pl.BlockSpec((HALO, D_MODEL), lambda i, f, t: (jnp.maximum(i * per - 1, 0), 0)),
            pl.BlockSpec((HALO, D_MODEL), lambda i, f, t: (jnp.minimum((i + 1) * per, last_halo), 0)),
            pl.BlockSpec((1, D_MODEL), lambda i, f, t: (0, 0)),
            pl.BlockSpec((D_MODEL, TF), lambda i, f, t: (0, f)),
            pl.BlockSpec((D_MODEL, TF), lambda i, f, t: (0, nf + f)),
            pl.BlockSpec((3, TF), lambda i, f, t: (0, f)),
            pl.BlockSpec((3, TF), lambda i, f, t: (0, nf + f)),
            pl.BlockSpec((1, TF), lambda i, f, t: (0, f)),
            pl.BlockSpec((1, TF), lambda i, f, t: (0, nf + f)),
            pl.BlockSpec((TF, D_MODEL), lambda i, f, t: (f, 0)),
            pl.BlockSpec((1, D_MODEL), lambda i, f, t: (0, 0)),
        ],
        out_specs=pl.BlockSpec((TM_FFN, D_MODEL), lambda i, f, t: (i, 0)),
        scratch_shapes=[pltpu.VMEM((TM_FFN + 2 * HALO, D_MODEL), BF16)],
    )
    return pl.pallas_call(
        functools.partial(_ffn_kernel, final=final),
        grid_spec=grid_spec,
        out_shape=jax.ShapeDtypeStruct((rows, D_MODEL), F32),
        compiler_params=pltpu.CompilerParams(
            dimension_semantics=("arbitrary", "arbitrary"), vmem_limit_bytes=VMEM_LIMIT),
        name="ffn",
    )(thr, x, x, x, gain, w_up, w_up, conv_w, conv_w, conv_b, conv_b, w_down, final_gain)


def _pack_sequences(x, meta_tokens):
    b = x.shape[0]
    pad = jnp.zeros((b, PAD_ROWS, D_MODEL), x.dtype)
    meta = jnp.broadcast_to(meta_tokens.astype(x.dtype)[None], (b, N_META, D_MODEL))
    return jnp.concatenate([pad, meta, x], axis=1).reshape(-1, D_MODEL)


def _unpack_sequences(h, first_blk, batch, nb_seq):
    n_blocks = h.shape[0] // BLK
    if batch == 1:
        return h[(first_blk + 1) * BLK:(first_blk + nb_seq) * BLK][None]
    if first_blk % nb_seq == 0 and n_blocks % nb_seq == 0:
        slots = h.reshape(n_blocks // nb_seq, nb_seq, BLK, D_MODEL)
        first = first_blk // nb_seq
        return slots[first:first + batch, 1:].reshape(batch, (nb_seq - 1) * BLK, D_MODEL)
    rows = h[first_blk * BLK:(first_blk + batch * nb_seq) * BLK]
    return rows.reshape(batch, nb_seq * BLK, D_MODEL)[:, BLK:]


def _in_weights(w):
    q = w[:, :ATTN_WIDTH] * (HEAD_DIM ** -0.5 * LOG2_E)
    k = w[:, ATTN_WIDTH:ATTN_WIDTH + KV_WIDTH].reshape(D_MODEL, ATTN_KV_HEADS, 1, HEAD_DIM)
    v = w[:, ATTN_WIDTH + KV_WIDTH:ATTN_WIDTH + 2 * KV_WIDTH].reshape(D_MODEL, ATTN_KV_HEADS, 1, HEAD_DIM)
    kk = jnp.broadcast_to(k, (D_MODEL, ATTN_KV_HEADS, 2, HEAD_DIM)).reshape(D_MODEL, KK_WIDTH)
    vv = jnp.broadcast_to(v, (D_MODEL, ATTN_KV_HEADS, 2, HEAD_DIM)).reshape(D_MODEL, KK_WIDTH)
    hg = w[:, ATTN_WIDTH + 2 * KV_WIDTH:]
    plain = jnp.concatenate([q, kk, vv, hg[:, :HG_WIDTH], hg[:, 3 * HG_WIDTH:]], axis=1)
    return plain.astype(BF16), hg[:, HG_WIDTH:3 * HG_WIDTH].astype(BF16)


def _lower_bound_params(lb_param, layer):
    c = jnp.cumsum(jax.nn.softmax(lb_param.astype(F32), axis=1), axis=1)
    lb = (c - c[:, :1])[:, layer].reshape(1, -1)
    rows = [jnp.log(lb), jnp.log1p(-lb), 1.0 - lb] + [jnp.zeros_like(lb)] * 5
    return jnp.concatenate(rows, axis=0)


def kernel(x_prompt, x_sample, meta_tokens, mix_norm, w_in, attn_sink, attn_out_norm, hgrn_lower_bounds,
           hgrn_out_norm, w_out, ffn_norm, w_up, conv_w, conv_b, w_down, final_norm):
    bp, sp, _ = x_prompt.shape
    bs, ss, _ = x_sample.shape
    depth = w_in.shape[0]
    lay = _make_layout(bp, sp, bs, ss)
    tables = _block_tables(lay)
    thr = jnp.asarray(tables[0])
    bias = _attn_bias_tables()
    consts = _hgrn_consts()

    rows_s = bs * lay.nb_s * BLK
    gap = lay.base_p * BLK - rows_s
    tail = lay.n_blocks * BLK - (lay.base_p + bp * lay.nb_p) * BLK
    parts = [_pack_sequences(x_sample, meta_tokens)]
    if gap:
        parts.append(jnp.zeros((gap, D_MODEL), F32))
    parts.append(_pack_sequences(x_prompt, meta_tokens))
    if tail:
        parts.append(jnp.zeros((tail, D_MODEL), F32))
    h = jnp.concatenate(parts, axis=0)
    rows = h.shape[0]

    row2 = lambda t: t.reshape(1, -1).astype(F32)
    for l in range(depth):
        w_plain, w_z = _in_weights(w_in[l])
        proj, lf, kg = _inproj(h, row2(mix_norm[l]), w_plain, w_z, _lower_bound_params(hgrn_lower_bounds, l))
        attn = _attention(proj, tables, bias, attn_sink[l].astype(F32) * LOG2_E, row2(attn_out_norm[l]))
        gn = row2(hgrn_out_norm[l])
        rec = jnp.zeros((rows, HG_WIDTH), BF16)
        rec = _hgrn_group(proj, lf, kg, rec, gn, consts, base_blk=0, batch=bs, nb_seq=lay.nb_s)
        rec = _hgrn_group(proj, lf, kg, rec, gn, consts, base_blk=lay.base_p, batch=bp, nb_seq=lay.nb_p)
        wo = w_out[l].astype(BF16)
        h = _outproj(thr, attn, rec, h, wo[:ATTN_WIDTH], wo[ATTN_WIDTH:])
        h = _ffn(thr, h, row2(ffn_norm[l]), w_up[l].astype(BF16), conv_w[l].astype(F32),
                 row2(conv_b[l]), w_down[l].astype(BF16), row2(final_norm), final=(l == depth - 1))

    return (_unpack_sequences(h, lay.base_p, bp, lay.nb_p), _unpack_sequences(h, 0, bs, lay.nb_s))
```

```python
import functools
from typing import NamedTuple

import numpy as np
import jax
import jax.numpy as jnp
from jax import lax
from jax.experimental import pallas as pl
from jax.experimental.pallas import tpu as pltpu

F32 = jnp.float32
BF16 = jnp.bfloat16

D_MODEL = 2048
N_META = 16
HEAD_DIM = 64
ATTN_WIDTH = 1024
ATTN_HEADS = 16
ATTN_KV_HEADS = 4
KV_WIDTH = ATTN_KV_HEADS * HEAD_DIM
BLK = 128
PAD_ROWS = BLK - N_META
HG_HEADS = 8
HG_DIM = 128
HG_WIDTH = HG_HEADS * HG_DIM
HG_PER_STEP = 4
HG_CHUNKS_PER_BODY = 3
HG_MAX_SEQ_ROWS = 4096
HG_MAX_BLOCKS_PER_STEP = 4
D_FF = 5632
EPS = 1e-6
NEG_INF = -1e30
LOG2_E = 1.4426950408889634
KK_WIDTH = ATTN_KV_HEADS * 128
QKV_COLS = ATTN_WIDTH + 2 * KK_WIDTH
IN_COLS2 = QKV_COLS + 5 * HG_WIDTH

TM = 1024
TM_RES = 512
TM_FFN = 768
TN_IN = 1792
TF = 512
HALO = 8
VMEM_LIMIT = 56 * 1024 * 1024


def _dot(a, b):
    return jnp.dot(a, b, preferred_element_type=F32)


def _dot_nt(a, b):
    return lax.dot_general(a, b, (((1,), (1,)), ((), ())), preferred_element_type=F32)


class _Layout(NamedTuple):
    nb_s: int
    nb_p: int
    base_p: int
    n_blocks: int
    seqs: tuple


def _make_layout(bp, sp, bs, ss):
    nb_s, nb_p = ss // BLK + 1, sp // BLK + 1
    base_p = -(-(bs * nb_s) // nb_p) * nb_p
    used = base_p + bp * nb_p
    per_tile = int(np.lcm.reduce([TM // BLK, TM_RES // BLK, TM_FFN // BLK]))
    n_blocks = -(-used // per_tile) * per_tile
    seqs = tuple((b * nb_s, nb_s) for b in range(bs)) + tuple((base_p + b * nb_p, nb_p) for b in range(bp))
    return _Layout(nb_s, nb_p, base_p, n_blocks, seqs)


def _block_tables(lay):
    n = lay.n_blocks
    thr = np.full((n,), BLK, np.int32)
    prev = np.arange(n, dtype=np.int32)
    nxt = np.arange(n, dtype=np.int32)
    meta = np.arange(n, dtype=np.int32)
    flags = np.zeros((n,), np.int32)
    for first, cnt in lay.seqs:
        for i in range(cnt):
            blk = first + i
            thr[blk] = PAD_ROWS if i == 0 else 0
            prev[blk] = max(blk - 1, first)
            nxt[blk] = min(blk + 1, first + cnt - 1)
            meta[blk] = first
            flags[blk] = (1 if i >= 2 else 0) | (2 if i >= 1 else 0) | (4 if i + 1 <= cnt - 1 else 0)
    return thr, prev, nxt, meta, flags


def _inproj_kernel(x_ref, g_ref, w_ref, o_ref, xn_ref):
    @pl.when(pl.program_id(1) == 0)
    def _():
        x = x_ref[...]
        ms = jnp.mean(x * x, axis=-1, keepdims=True)
        xn_ref[...] = (x * lax.rsqrt(ms + EPS) * g_ref[...]).astype(BF16)

    o_ref[...] = _dot(xn_ref[...], w_ref[...]).astype(BF16)


def _inproj(x, gain, w):
    rows = x.shape[0]
    return pl.pallas_call(
        _inproj_kernel,
        grid=(rows // TM, IN_COLS2 // TN_IN),
        in_specs=[
            pl.BlockSpec((TM, D_MODEL), lambda i, j: (i, 0)),
            pl.BlockSpec((1, D_MODEL), lambda i, j: (0, 0)),
            pl.BlockSpec((D_MODEL, TN_IN), lambda i, j: (0, j)),
        ],
        out_specs=pl.BlockSpec((TM, TN_IN), lambda i, j: (i, j)),
        out_shape=jax.ShapeDtypeStruct((rows, IN_COLS2), BF16),
        scratch_shapes=[pltpu.VMEM((TM, D_MODEL), BF16)],
        compiler_params=pltpu.CompilerParams(
            dimension_semantics=("arbitrary", "arbitrary"), vmem_limit_bytes=VMEM_LIMIT),
        name="inproj",
    )(x, gain, w)


HALF = BLK // 2
N_KEYS = 3 * BLK
KEY_PAD = N_KEYS - (2 * BLK + HALF + N_META)


def _attn_bias_tables():
    slopes = (2.0 ** (-8.0 * np.arange(1, ATTN_HEADS + 1) / ATTN_HEADS)).astype(np.float32)
    full = np.arange(BLK)
    dists, oks, groups = [], [], []
    for i, prev_j, next_j in ((np.arange(HALF), full, full[:HALF]), (HALF + np.arange(HALF), full[HALF:], full)):
        i = i[:, None]
        d_prev = BLK + i - prev_j[None]
        d_cur = np.abs(i - full[None])
        d_next = BLK + next_j[None] - i
        tail = np.zeros((HALF, N_META + KEY_PAD), np.int64)
        dists.append(np.concatenate([d_prev, d_cur, d_next, tail], axis=1))
        ok_tail = np.concatenate([np.ones((HALF, N_META), bool), np.zeros((HALF, KEY_PAD), bool)], axis=1)
        oks.append(np.concatenate([d_prev <= BLK, d_cur <= BLK, d_next <= BLK, ok_tail], axis=1))
        groups.append(np.concatenate([np.full(len(prev_j), 0), np.full(BLK, 1), np.full(len(next_j), 2),
                                      np.full(N_META + KEY_PAD, 3)]))
    dist = np.concatenate(dists, axis=0).astype(np.float32)
    ok = np.concatenate(oks, axis=0)
    group = np.stack(groups, axis=0).repeat(HALF, axis=0)
    flags = np.arange(8)[:, None, None]
    group_ok = np.where(group[None] == 3, True, ((flags >> np.minimum(group[None], 2)) & 1) > 0)
    valid = jnp.asarray(group_ok & ok[None])
    base = -(jnp.asarray(slopes)[:, None, None] * jnp.asarray(dist)[None]) * LOG2_E
    return jnp.where(valid[:, None], base[None], NEG_INF).astype(F32)


def _attn_kernel(prev_t, next_t, meta_t, flag_t, q_ref, kp_ref, kc_ref, kn_ref, km_ref,
                 t_ref, sink_ref, gain_ref, o_ref, acc_ref, kk_ref, vv_ref):
    del prev_t, next_t, meta_t
    lo = lax.broadcasted_iota(jnp.int32, (N_KEYS, BLK), 1) < HEAD_DIM
    lo_q = lax.broadcasted_iota(jnp.int32, (HALF, BLK), 1) < HEAD_DIM
    zero = jnp.zeros((), BF16)
    pad = jnp.zeros((KEY_PAD, BLK), BF16)
    ones_lo = jnp.where(lo, 1.0, 0.0).astype(BF16)
    for kh in range(ATTN_KV_HEADS):
        for half in range(2):
            vv_ref[kh, half, :N_KEYS, BLK:] = ones_lo
            vv_ref[kh, half, N_KEYS:, BLK:] = 1.0 - ones_lo

    def stack_keys(kh):
        kcols = slice(kh * BLK, (kh + 1) * BLK)
        vcols = slice(KK_WIDTH + kh * BLK, KK_WIDTH + (kh + 1) * BLK)
        for half in range(2):
            def keys(c):
                if half == 0:
                    parts = [kp_ref[:, c], kc_ref[:, c], kn_ref[:HALF, c]]
                else:
                    parts = [kp_ref[HALF:, c], kc_ref[:, c], kn_ref[:, c]]
                return jnp.concatenate(parts + [km_ref[PAD_ROWS:, c], pad], axis=0)

            kcat, vcat = keys(kcols), keys(vcols)
            kk_ref[kh, half, :N_KEYS, :] = jnp.where(lo, kcat, zero)
            kk_ref[kh, half, N_KEYS:, :] = jnp.where(lo, zero, kcat)
            vv_ref[kh, half, :N_KEYS, :BLK] = jnp.where(lo, vcat, zero)
            vv_ref[kh, half, N_KEYS:, :BLK] = jnp.where(lo, zero, vcat)

    def units(kh):
        return [(kh, half) for half in range(2)]

    def scores(kh, half):
        rows = slice(half * HALF, (half + 1) * HALF)
        q2 = jnp.concatenate([q_ref[rows, (2 * kh + j) * BLK:(2 * kh + j + 1) * BLK] for j in range(2)], axis=0)
        return _dot_nt(q2, kk_ref[kh, half])

    def softmax(s, kh, half):
        rows = slice(half * HALF, (half + 1) * HALF)
        ps, sinks = [], []
        for j in range(2):
            h_e = 4 * kh + 2 * j
            sj = s[j * HALF:(j + 1) * HALF, :]
            s_e = sj[:, :N_KEYS] + t_ref[0, h_e, rows, :]
            s_o = sj[:, N_KEYS:] + t_ref[0, h_e + 1, rows, :]
            m_e = jnp.maximum(jnp.max(s_e, axis=-1, keepdims=True), sink_ref[h_e])
            m_o = jnp.maximum(jnp.max(s_o, axis=-1, keepdims=True), sink_ref[h_e + 1])
            ps.append(jnp.concatenate([jnp.exp2(s_e - m_e), jnp.exp2(s_o - m_o)], axis=1).astype(BF16))
            sinks.append(jnp.where(lo_q, jnp.exp2(sink_ref[h_e] - m_e), jnp.exp2(sink_ref[h_e + 1] - m_o)))
        return jnp.concatenate(ps, axis=0), sinks

    def weighted_values(p, sinks, kh, half):
        rows = slice(half * HALF, (half + 1) * HALF)
        o4 = _dot(p, vv_ref[kh, half])
        for j in range(2):
            oj = o4[j * HALF:(j + 1) * HALF, :]
            col = (2 * kh + j) * BLK
            acc_ref[rows, col:col + BLK] = oj[:, :BLK] / (oj[:, BLK:] + sinks[j])

    for kh in range(ATTN_KV_HEADS):
        stack_keys(kh)
    pending = None
    for kh in range(ATTN_KV_HEADS + 1):
        s_now = [scores(*u) for u in units(kh)] if kh < ATTN_KV_HEADS else None
        if pending is not None:
            probs = [softmax(s, *u) for s, u in zip(pending, units(kh - 1))]
            for (p, sinks), u in zip(probs, units(kh - 1)):
                weighted_values(p, sinks, *u)
        pending = s_now

    o = acc_ref[...]
    ms = jnp.mean(o * o, axis=-1, keepdims=True)
    o_ref[...] = (o * lax.rsqrt(ms + EPS) * gain_ref[...]).astype(BF16)


def _attention(qkv, tables, bias, sink, gain):
    _, prev, nxt, meta, flags = tables
    n_blocks = qkv.shape[0] // BLK
    kv_spec = lambda f: pl.BlockSpec((BLK, 2 * KK_WIDTH), f)
    grid_spec = pltpu.PrefetchScalarGridSpec(
        num_scalar_prefetch=4,
        grid=(n_blocks,),
        in_specs=[
            pl.BlockSpec((BLK, ATTN_WIDTH), lambda i, p, n, m, f: (i, 0)),
            kv_spec(lambda i, p, n, m, f: (p[i], 1)),
            kv_spec(lambda i, p, n, m, f: (i, 1)),
            kv_spec(lambda i, p, n, m, f: (n[i], 1)),
            kv_spec(lambda i, p, n, m, f: (m[i], 1)),
            pl.BlockSpec((1, ATTN_HEADS, BLK, N_KEYS), lambda i, p, n, m, f: (f[i], 0, 0, 0)),
            pl.BlockSpec(memory_space=pltpu.SMEM),
            pl.BlockSpec((1, ATTN_WIDTH), lambda i, p, n, m, f: (0, 0)),
        ],
        out_specs=pl.BlockSpec((BLK, ATTN_WIDTH), lambda i, p, n, m, f: (i, 0)),
        scratch_shapes=[
            pltpu.VMEM((BLK, ATTN_WIDTH), F32),
            pltpu.VMEM((ATTN_KV_HEADS, 2, 2 * N_KEYS, BLK), BF16),
            pltpu.VMEM((ATTN_KV_HEADS, 2, 2 * N_KEYS, 2 * BLK), BF16),
        ],
    )
    return pl.pallas_call(
        _attn_kernel,
        grid_spec=grid_spec,
        out_shape=jax.ShapeDtypeStruct((qkv.shape[0], ATTN_WIDTH), BF16),
        compiler_params=pltpu.CompilerParams(
            dimension_semantics=("arbitrary",), vmem_limit_bytes=VMEM_LIMIT),
        name="attention",
    )(jnp.asarray(prev), jnp.asarray(nxt), jnp.asarray(meta), jnp.asarray(flags),
      qkv, qkv, qkv, qkv, qkv, bias, sink, gain)


_LEVELS = (64, 32, 16, 8, 4, 2, 1)


def _hgrn_consts():
    t = np.arange(BLK)[:, None]
    s = np.arange(BLK)[None, :]
    lmat = (s <= t).astype(np.float32)
    umat = (s >= t).astype(np.float32)
    x = t ^ s
    lv = np.where(x == 0, -1, np.floor(np.log2(np.maximum(x, 1)))).astype(np.int32)
    return jnp.asarray(lmat, BF16), jnp.asarray(umat, BF16), jnp.asarray(lv)


def _level_reference(ac_ref, c, rev):
    off = c - 1 + (1 if rev else 0)
    if c == 4:
        pieces = [jnp.broadcast_to(ac_ref[bs + off:bs + off + 1, :], (8, HG_DIM)) for bs in range(0, BLK, 8)]
        return jnp.concatenate(pieces, axis=0)
    assert c == 2
    sub = lax.broadcasted_iota(jnp.int32, (8, HG_DIM), 0)
    pieces = []
    for bs in range(0, BLK, 8):
        r1 = jnp.broadcast_to(ac_ref[bs + off:bs + off + 1, :], (8, HG_DIM))
        r2 = jnp.broadcast_to(ac_ref[bs + 4 + off:bs + 4 + off + 1, :], (8, HG_DIM))
        pieces.append(jnp.where(sub < 4, r1, r2))
    return jnp.concatenate(pieces, axis=0)


def _upper_level_operands(q, k, ac, ac_ref, c, rev):
    zeros = jnp.zeros((c, HG_DIM), F32)
    q_parts, k_parts = [], []
    for bs in range(0, BLK, 2 * c):
        lo_rows, hi_rows = slice(bs, bs + c), slice(bs + c, bs + 2 * c)
        q_rows, k_rows = (lo_rows, hi_rows) if rev else (hi_rows, lo_rows)
        r = bs + c if rev else bs + c - 1
        ref = ac_ref[r:r + 1, :]
        qb = q[q_rows] * jnp.exp2(ac[q_rows] - ref)
        kb = k[k_rows] * jnp.exp2(ref - ac[k_rows])
        q_parts += [qb, zeros] if rev else [zeros, qb]
        k_parts += [zeros, kb] if rev else [kb, zeros]
    return jnp.concatenate(q_parts, axis=0).astype(BF16), jnp.concatenate(k_parts, axis=0).astype(BF16)


def _hgrn_chunk(qs, zs, vs, lbps, cmat, lv, ac_refs, st_refs, rev):
    n = len(qs)
    off = 3 if rev else 0
    row = lax.broadcasted_iota(jnp.int32, (BLK, HG_DIM), 0)

    lfs, ks, acs = [], [], []
    for z, lbp, ac_ref in zip(zs, lbps, ac_refs):
        log_lb, log_1m_lb, one_m_lb = lbp[off:off + 1, :], lbp[off + 1:off + 2, :], lbp[off + 2:off + 3, :]
        log_sig = jnp.minimum(z, 0.0) - jnp.log(1.0 + jnp.exp(-jnp.abs(z)))
        bv = log_1m_lb + log_sig
        lf = jnp.maximum(log_lb, bv) + jnp.log(1.0 + jnp.exp(-jnp.abs(log_lb - bv)))
        k = one_m_lb * jnp.exp(log_sig - z)
        lf = lf * LOG2_E
        hi = lf.astype(BF16)
        lo = (lf - hi.astype(F32)).astype(BF16)
        ac2 = _dot(cmat, jnp.concatenate([hi, lo], axis=1))
        ac = ac2[:, :HG_DIM] + ac2[:, HG_DIM:]
        ac_ref[...] = ac
        lfs.append(lf)
        ks.append(k)
        acs.append(ac)

    ps = [None] * n
    for c in _LEVELS:
        q_rows = ((row & c) == 0) if rev else ((row & c) != 0)
        sign = jnp.where(q_rows, 1.0, -1.0)
        in_level = lv == int(np.log2(c))
        for i in range(n):
            q, k, ac = qs[i], ks[i], acs[i]
            if c >= 8:
                qt, kt = _upper_level_operands(q, k, ac, ac_refs[i], c, rev)
            else:
                if c == 1:
                    w = jnp.where(q_rows, jnp.exp2(lfs[i]), 1.0)
                else:
                    w = jnp.exp2((ac - _level_reference(ac_refs[i], c, rev)) * sign)
                qt = jnp.where(q_rows, q * w, 0.0).astype(BF16)
                kt = jnp.where(q_rows, 0.0, k * w).astype(BF16)
            pc = _dot_nt(qt, kt).astype(BF16)
            ps[i] = pc if ps[i] is None else jnp.where(in_level, pc, ps[i])

    outs = []
    for i in range(n):
        diag = jnp.sum(qs[i] * ks[i], axis=-1, keepdims=True).astype(BF16)
        p = jnp.where(lv == -1, diag, ps[i])
        st = st_refs[i][...]
        outs.append(_dot(p, vs[i]) + _dot_nt((qs[i] * jnp.exp2(acs[i])).astype(BF16), st.astype(BF16)))
    for i in range(n):
        edge = ac_refs[i][0:1, :] if rev else ac_refs[i][BLK - 1:BLK, :]
        ke = (ks[i] * jnp.exp2(edge - acs[i])).astype(BF16)
        st_refs[i][...] = st_refs[i][...] * jnp.exp2(edge) + _dot(vs[i].T, ke)
    return outs


def _hgrn_kernel(q_ref, zf_ref, zb_ref, v_ref, g_ref, lbp_ref, gn_ref, lmat_ref, umat_ref, lv_ref,
                 alias_ref, o_ref, st_ref, obuf_ref, ac_ref, *, nrb, tb):
    del alias_ref
    ph = pl.program_id(2)
    rb = pl.program_id(3)
    nch = tb // BLK
    lv = lv_ref[...]

    @pl.when(rb == 0)
    def _():
        st_ref[...] = jnp.zeros_like(st_ref)

    def sweep(rev):
        z_ref = zb_ref if rev else zf_ref
        cmat = umat_ref[...] if rev else lmat_ref[...]
        rb_seq = (nrb - 1 - rb) if rev else rb

        def chunk(ci, parity):
            cj = (nch - 1 - ci) if rev else ci
            r0 = pl.multiple_of(cj * BLK, BLK)
            rows = pl.ds(r0, BLK)
            seq_rows = pl.ds(pl.multiple_of(rb_seq * tb + r0, BLK), BLK)
            heads = range(HG_PER_STEP)
            col = [slice(hh * HG_DIM, (hh + 1) * HG_DIM) for hh in heads]
            outs = _hgrn_chunk([q_ref[rows, c].astype(F32) for c in col],
                               [z_ref[rows, c].astype(F32) for c in col],
                               [v_ref[rows, c] for c in col],
                               [lbp_ref.at[:, c] for c in col], cmat, lv,
                               [ac_ref.at[parity, hh] for hh in heads], [st_ref.at[hh] for hh in heads], rev)
            for cols, o in zip(col, outs):
                if rev:
                    obuf_ref[seq_rows, cols] = o
                else:
                    o = o + obuf_ref[seq_rows, cols]
                    ms = jnp.mean(o * o, axis=-1, keepdims=True)
                    y = o * lax.rsqrt(ms + EPS) * gn_ref[:, cols]
                    g = g_ref[rows, cols].astype(F32)
                    o_ref[rows, cols] = (y * (g / (1.0 + jnp.exp(-g)))).astype(BF16)

        def body(bi, carry):
            for u in range(HG_CHUNKS_PER_BODY):
                chunk(HG_CHUNKS_PER_BODY * bi + u, u)
            return carry

        n_bodies = nch // HG_CHUNKS_PER_BODY
        lax.fori_loop(0, n_bodies, body, 0)
        for u in range(nch % HG_CHUNKS_PER_BODY):
            chunk(n_bodies * HG_CHUNKS_PER_BODY + u, u)

    @pl.when(ph == 0)
    def _():
        sweep(True)

    @pl.when(ph == 1)
    def _():
        sweep(False)


def _hgrn_group(hg, rec, lbp, gn, consts, *, base_blk, batch, nb_seq):
    if nb_seq * BLK <= HG_MAX_SEQ_ROWS:
        kmax = nb_seq
    else:
        kmax = max(k for k in range(1, HG_MAX_BLOCKS_PER_STEP + 1) if nb_seq % k == 0)
    tb = kmax * BLK
    nrb = nb_seq // kmax
    assert (base_blk * BLK) % tb == 0
    base = base_blk * BLK // tb
    lmat, umat, lv = consts

    def rows_in(b, h, ph, rb):
        return base + b * nrb + jnp.where(ph == 0, nrb - 1 - rb, rb)

    def rows_fwd_only(b, h, ph, rb):
        return base + b * nrb + jnp.where(ph == 0, 0, rb)

    def rows_bwd_only(b, h, ph, rb):
        return base + b * nrb + jnp.where(ph == 0, nrb - 1 - rb, 0)

    width = HG_PER_STEP * HG_DIM
    groups = HG_HEADS // HG_PER_STEP
    col_base = QKV_COLS // width
    blk = lambda rows_fn, col0: pl.BlockSpec(
        (tb, width), lambda b, h, ph, rb: (rows_fn(b, h, ph, rb), col0 + h))
    const = lambda shape: pl.BlockSpec(shape, lambda b, h, ph, rb: (0, 0))
    seq_rows = nb_seq * BLK
    return pl.pallas_call(
        functools.partial(_hgrn_kernel, nrb=nrb, tb=tb),
        grid=(batch, groups, 2, nrb),
        in_specs=[
            blk(rows_in, col_base),
            blk(rows_fwd_only, col_base + groups),
            blk(rows_bwd_only, col_base + 2 * groups),
            blk(rows_in, col_base + 3 * groups),
            blk(rows_fwd_only, col_base + 4 * groups),
            pl.BlockSpec((8, width), lambda b, h, ph, rb: (0, h)),
            pl.BlockSpec((1, width), lambda b, h, ph, rb: (0, h)),
            const((BLK, BLK)), const((BLK, BLK)), const((BLK, BLK)),
            pl.BlockSpec(memory_space=pl.ANY),
        ],
        out_specs=blk(rows_fwd_only, 0),
        out_shape=jax.ShapeDtypeStruct(rec.shape, rec.dtype),
        input_output_aliases={10: 0},
        scratch_shapes=[
            pltpu.VMEM((HG_PER_STEP, HG_DIM, HG_DIM), F32),
            pltpu.VMEM((seq_rows, width), F32),
            pltpu.VMEM((HG_CHUNKS_PER_BODY, HG_PER_STEP, BLK, HG_DIM), F32),
        ],
        compiler_params=pltpu.CompilerParams(
            dimension_semantics=("arbitrary",) * 4, vmem_limit_bytes=VMEM_LIMIT),
        name="hgrn",
    )(hg, hg, hg, hg, hg, lbp, gn, lmat, umat, lv, rec)


def _store_valid_rows(o_ref, y, thr_ref, tile, rows_per_tile):
    row = lax.broadcasted_iota(jnp.int32, (BLK, y.shape[1]), 0)
    for kb in range(rows_per_tile // BLK):
        t = thr_ref[tile * (rows_per_tile // BLK) + kb]
        o_ref[kb * BLK:(kb + 1) * BLK, :] = jnp.where(row >= t, y[kb * BLK:(kb + 1) * BLK, :], 0.0)


def _outproj_kernel(thr_ref, a_ref, r_ref, x_ref, wa_ref, wr_ref, o_ref):
    y = x_ref[...] + _dot(a_ref[...], wa_ref[...]) + _dot(r_ref[...], wr_ref[...])
    _store_valid_rows(o_ref, y, thr_ref, pl.program_id(0), TM_RES)


def _outproj(thr, attn, rec, x, wa, wr):
    rows = x.shape[0]
    grid_spec = pltpu.PrefetchScalarGridSpec(
        num_scalar_prefetch=1,
        grid=(rows // TM_RES,),
        in_specs=[
            pl.BlockSpec((TM_RES, ATTN_WIDTH), lambda i, t: (i, 0)),
            pl.BlockSpec((TM_RES, HG_WIDTH), lambda i, t: (i, 0)),
            pl.BlockSpec((TM_RES, D_MODEL), lambda i, t: (i, 0)),
            pl.BlockSpec((ATTN_WIDTH, D_MODEL), lambda i, t: (0, 0)),
            pl.BlockSpec((HG_WIDTH, D_MODEL), lambda i, t: (0, 0)),
        ],
        out_specs=pl.BlockSpec((TM_RES, D_MODEL), lambda i, t: (i, 0)),
    )
    return pl.pallas_call(
        _outproj_kernel,
        grid_spec=grid_spec,
        out_shape=jax.ShapeDtypeStruct((rows, D_MODEL), F32),
        compiler_params=pltpu.CompilerParams(
            dimension_semantics=("arbitrary",), vmem_limit_bytes=VMEM_LIMIT),
        name="outproj",
    )(thr, attn, rec, x, wa, wr)


def _ffn_kernel(thr_ref, x_ref, xp_ref, xn_ref, g_ref, wa_ref, wb_ref, cwa_ref, cwb_ref, cba_ref, cbb_ref,
                wd_ref, gf_ref, o_ref, xs_ref, *, final):
    i = pl.program_id(0)
    f = pl.program_id(1)
    tm = x_ref.shape[0]
    ext = tm + 2 * HALO

    def norm(x):
        ms = jnp.mean(x * x, axis=-1, keepdims=True)
        return (x * lax.rsqrt(ms + EPS) * g_ref[...]).astype(BF16)

    @pl.when(f == 0)
    def _():
        xs_ref[HALO:HALO + tm, :] = norm(x_ref[...])
        xs_ref[0:HALO, :] = norm(xp_ref[...])
        nxt = jnp.where(i == pl.num_programs(0) - 1, 0.0, xn_ref[...])
        xs_ref[HALO + tm:ext, :] = norm(nxt)
        o_ref[...] = jnp.zeros_like(o_ref)

    xs = xs_ref[...]

    def conv(w_ref, cw_ref, cb_ref):
        u = _dot(xs, w_ref[...])
        u_prev = pltpu.roll(u, 1, 0)[HALO:HALO + tm, :]
        u_next = pltpu.roll(u, ext - 1, 0)[HALO:HALO + tm, :]
        cw = cw_ref[...]
        return cw[0:1, :] * u_prev + cw[1:2, :] * u[HALO:HALO + tm, :] + cw[2:3, :] * u_next + cb_ref[...]

    ca = conv(wa_ref, cwa_ref, cba_ref)
    cb = conv(wb_ref, cwb_ref, cbb_ref)
    act = (ca / (1.0 + jnp.exp(-ca))) * cb
    o_ref[...] += _dot(act.astype(BF16), wd_ref[...])

    @pl.when(f == pl.num_programs(1) - 1)
    def _():
        y = x_ref[...] + o_ref[...]
        if final:
            ms = jnp.mean(y * y, axis=-1, keepdims=True)
            y = y * lax.rsqrt(ms + EPS) * gf_ref[...]
        _store_valid_rows(o_ref, y, thr_ref, i, tm)


def _ffn(thr, x, gain, w_up, conv_w, conv_b, w_down, final_gain, *, final):
    rows = x.shape[0]
    nf = D_FF // TF
    per = TM_FFN // HALO
    last_halo = rows // HALO - 1
    grid_spec = pltpu.PrefetchScalarGridSpec(
        num_scalar_prefetch=1,
        grid=(rows // TM_FFN, nf),
        in_specs=[
            pl.BlockSpec((TM_FFN, D_MODEL), lambda i, f, t: (i, 0)),
            pl.BlockSpec((HALO, D_MODEL), lambda i, f, t: (jnp.maximum(i * per - 1, 0), 0)),
            pl.BlockSpec((HALO, D_MODEL), lambda i, f, t: (jnp.minimum((i + 1) * per, last_halo), 0)),
            pl.BlockSpec((1, D_MODEL), lambda i, f, t: (0, 0)),
            pl.BlockSpec((D_MODEL, TF), lambda i, f, t: (0, f)),
            pl.BlockSpec((D_MODEL, TF), lambda i, f, t: (0, nf + f)),
            pl.BlockSpec((3, TF), lambda i, f, t: (0, f)),
            pl.BlockSpec((3, TF), lambda i, f, t: (0, nf + f)),
            pl.BlockSpec((1, TF), lambda i, f, t: (0, f)),
            pl.BlockSpec((1, TF), lambda i, f, t: (0, nf + f)),
            pl.BlockSpec((TF, D_MODEL), lambda i, f, t: (f, 0)),
            pl.BlockSpec((1, D_MODEL), lambda i, f, t: (0, 0)),
        ],
        out_specs=pl.BlockSpec((TM_FFN, D_MODEL), lambda i, f, t: (i, 0)),
        scratch_shapes=[pltpu.VMEM((TM_FFN + 2 * HALO, D_MODEL), BF16)],
    )
    return pl.pallas_call(
        functools.partial(_ffn_kernel, final=final),
        grid_spec=grid_spec,
        out_shape=jax.ShapeDtypeStruct((rows, D_MODEL), F32),
        compiler_params=pltpu.CompilerParams(
            dimension_semantics=("arbitrary", "arbitrary"), vmem_limit_bytes=VMEM_LIMIT),
        name="ffn",
    )(thr, x, x, x, gain, w_up, w_up, conv_w, conv_w, conv_b, conv_b, w_down, final_gain)


def _pack_sequences(x, meta_tokens):
    b = x.shape[0]
    pad = jnp.zeros((b, PAD_ROWS, D_MODEL), x.dtype)
    meta = jnp.broadcast_to(meta_tokens.astype(x.dtype)[None], (b, N_META, D_MODEL))
    return jnp.concatenate([pad, meta, x], axis=1).reshape(-1, D_MODEL)


def _unpack_sequences(h, first_blk, batch, nb_seq):
    n_blocks = h.shape[0] // BLK
    if batch == 1:
        return h[(first_blk + 1) * BLK:(first_blk + nb_seq) * BLK][None]
    if first_blk % nb_seq == 0 and n_blocks % nb_seq == 0:
        slots = h.reshape(n_blocks // nb_seq, nb_seq, BLK, D_MODEL)
        first = first_blk // nb_seq
        return slots[first:first + batch, 1:].reshape(batch, (nb_seq - 1) * BLK, D_MODEL)
    rows = h[first_blk * BLK:(first_blk + batch * nb_seq) * BLK]
    return rows.reshape(batch, nb_seq * BLK, D_MODEL)[:, BLK:]


def _in_weights(w):
    q = w[:, :ATTN_WIDTH] * (HEAD_DIM ** -0.5 * LOG2_E)
    k = w[:, ATTN_WIDTH:ATTN_WIDTH + KV_WIDTH].reshape(D_MODEL, ATTN_KV_HEADS, 1, HEAD_DIM)
    v = w[:, ATTN_WIDTH + KV_WIDTH:ATTN_WIDTH + 2 * KV_WIDTH].reshape(D_MODEL, ATTN_KV_HEADS, 1, HEAD_DIM)
    kk = jnp.broadcast_to(k, (D_MODEL, ATTN_KV_HEADS, 2, HEAD_DIM)).reshape(D_MODEL, KK_WIDTH)
    vv = jnp.broadcast_to(v, (D_MODEL, ATTN_KV_HEADS, 2, HEAD_DIM)).reshape(D_MODEL, KK_WIDTH)
    hg = w[:, ATTN_WIDTH + 2 * KV_WIDTH:]
    return jnp.concatenate([q, kk, vv, hg], axis=1).astype(BF16)


def _lower_bound_params(lb_param, layer):
    c = jnp.cumsum(jax.nn.softmax(lb_param.astype(F32), axis=1), axis=1)
    lb = (c - c[:, :1])[:, layer]
    rows = [jnp.log(lb[0]), jnp.log1p(-lb[0]), 1.0 - lb[0], jnp.log(lb[1]), jnp.log1p(-lb[1]), 1.0 - lb[1]]
    rows += [jnp.zeros_like(lb[0])] * 2
    return jnp.stack(rows, axis=0)


def kernel(x_prompt, x_sample, meta_tokens, mix_norm, w_in, attn_sink, attn_out_norm, hgrn_lower_bounds,
           hgrn_out_norm, w_out, ffn_norm, w_up, conv_w, conv_b, w_down, final_norm):
    bp, sp, _ = x_prompt.shape
    bs, ss, _ = x_sample.shape
    depth = w_in.shape[0]
    lay = _make_layout(bp, sp, bs, ss)
    tables = _block_tables(lay)
    thr = jnp.asarray(tables[0])
    bias = _attn_bias_tables()
    consts = _hgrn_consts()

    rows_s = bs * lay.nb_s * BLK
    gap = lay.base_p * BLK - rows_s
    tail = lay.n_blocks * BLK - (lay.base_p + bp * lay.nb_p) * BLK
    parts = [_pack_sequences(x_sample, meta_tokens)]
    if gap:
        parts.append(jnp.zeros((gap, D_MODEL), F32))
    parts.append(_pack_sequences(x_prompt, meta_tokens))
    if tail:
        parts.append(jnp.zeros((tail, D_MODEL), F32))
    h = jnp.concatenate(parts, axis=0)
    rows = h.shape[0]

    row2 = lambda t: t.reshape(1, -1).astype(F32)
    for l in range(depth):
        proj = _inproj(h, row2(mix_norm[l]), _in_weights(w_in[l]))
        attn = _attention(proj, tables, bias, attn_sink[l].astype(F32) * LOG2_E, row2(attn_out_norm[l]))
        lbp = _lower_bound_params(hgrn_lower_bounds, l)
        gn = row2(hgrn_out_norm[l])
        rec = jnp.zeros((rows, HG_WIDTH), BF16)
        rec = _hgrn_group(proj, rec, lbp, gn, consts, base_blk=0, batch=bs, nb_seq=lay.nb_s)
        rec = _hgrn_group(proj, rec, lbp, gn, consts, base_blk=lay.base_p, batch=bp, nb_seq=lay.nb_p)
        wo = w_out[l].astype(BF16)
        h = _outproj(thr, attn, rec, h, wo[:ATTN_WIDTH], wo[ATTN_WIDTH:])
        h = _ffn(thr, h, row2(ffn_norm[l]), w_up[l].astype(BF16), conv_w[l].astype(F32),
                 row2(conv_b[l]), w_down[l].astype(BF16), row2(final_norm), final=(l == depth - 1))

    return (_unpack_sequences(h, lay.base_p, bp, lay.nb_p), _unpack_sequences(h, 0, bs, lay.nb_s))
```

```python
import functools
from typing import NamedTuple

import numpy as np
import jax
import jax.numpy as jnp
from jax import lax
from jax.experimental import pallas as pl
from jax.experimental.pallas import tpu as pltpu

F32 = jnp.float32
BF16 = jnp.bfloat16

D_MODEL = 2048
N_META = 16
HEAD_DIM = 64
ATTN_WIDTH = 1024
ATTN_HEADS = 16
ATTN_KV_HEADS = 4
KV_WIDTH = ATTN_KV_HEADS * HEAD_DIM
BLK = 128
PAD_ROWS = BLK - N_META
HG_HEADS = 8
HG_DIM = 128
HG_WIDTH = HG_HEADS * HG_DIM
HG_PER_STEP = 4
HG_CHUNKS_PER_BODY = 3
HG_MAX_SEQ_ROWS = 4096
HG_MAX_BLOCKS_PER_STEP = 4
D_FF = 5632
EPS = 1e-6
NEG_INF = -1e30
LOG2_E = 1.4426950408889634
KK_WIDTH = ATTN_KV_HEADS * 128
QKV_COLS = ATTN_WIDTH + 2 * KK_WIDTH
IN_COLS2 = QKV_COLS + 5 * HG_WIDTH

TM = 1024
TM_RES = 512
TM_FFN = 768
TN_IN = 1792
TF = 512
HALO = 8
VMEM_LIMIT = 56 * 1024 * 1024


def _dot(a, b):
    return jnp.dot(a, b, preferred_element_type=F32)


def _dot_nt(a, b):
    return lax.dot_general(a, b, (((1,), (1,)), ((), ())), preferred_element_type=F32)


class _Layout(NamedTuple):
    nb_s: int
    nb_p: int
    base_p: int
    n_blocks: int
    seqs: tuple


def _make_layout(bp, sp, bs, ss):
    nb_s, nb_p = ss // BLK + 1, sp // BLK + 1
    base_p = -(-(bs * nb_s) // nb_p) * nb_p
    used = base_p + bp * nb_p
    per_tile = int(np.lcm.reduce([TM // BLK, TM_RES // BLK, TM_FFN // BLK]))
    n_blocks = -(-used // per_tile) * per_tile
    seqs = tuple((b * nb_s, nb_s) for b in range(bs)) + tuple((base_p + b * nb_p, nb_p) for b in range(bp))
    return _Layout(nb_s, nb_p, base_p, n_blocks, seqs)


def _block_tables(lay):
    n = lay.n_blocks
    thr = np.full((n,), BLK, np.int32)
    prev = np.arange(n, dtype=np.int32)
    nxt = np.arange(n, dtype=np.int32)
    meta = np.arange(n, dtype=np.int32)
    flags = np.zeros((n,), np.int32)
    for first, cnt in lay.seqs:
        for i in range(cnt):
            blk = first + i
            thr[blk] = PAD_ROWS if i == 0 else 0
            prev[blk] = max(blk - 1, first)
            nxt[blk] = min(blk + 1, first + cnt - 1)
            meta[blk] = first
            flags[blk] = (1 if i >= 2 else 0) | (2 if i >= 1 else 0) | (4 if i + 1 <= cnt - 1 else 0)
    return thr, prev, nxt, meta, flags


def _inproj_kernel(x_ref, g_ref, w_ref, o_ref, xn_ref):
    @pl.when(pl.program_id(1) == 0)
    def _():
        x = x_ref[...]
        ms = jnp.mean(x * x, axis=-1, keepdims=True)
        xn_ref[...] = (x * lax.rsqrt(ms + EPS) * g_ref[...]).astype(BF16)

    o_ref[...] = _dot(xn_ref[...], w_ref[...]).astype(BF16)


def _inproj(x, gain, w):
    rows = x.shape[0]
    return pl.pallas_call(
        _inproj_kernel,
        grid=(rows // TM, IN_COLS2 // TN_IN),
        in_specs=[
            pl.BlockSpec((TM, D_MODEL), lambda i, j: (i, 0)),
            pl.BlockSpec((1, D_MODEL), lambda i, j: (0, 0)),
            pl.BlockSpec((D_MODEL, TN_IN), lambda i, j: (0, j)),
        ],
        out_specs=pl.BlockSpec((TM, TN_IN), lambda i, j: (i, j)),
        out_shape=jax.ShapeDtypeStruct((rows, IN_COLS2), BF16),
        scratch_shapes=[pltpu.VMEM((TM, D_MODEL), BF16)],
        compiler_params=pltpu.CompilerParams(
            dimension_semantics=("arbitrary", "arbitrary"), vmem_limit_bytes=VMEM_LIMIT),
        name="inproj",
    )(x, gain, w)


HALF = BLK // 2
N_KEYS = 3 * BLK
KEY_PAD = N_KEYS - (2 * BLK + HALF + N_META)


def _attn_bias_tables():
    slopes = (2.0 ** (-8.0 * np.arange(1, ATTN_HEADS + 1) / ATTN_HEADS)).astype(np.float32)
    full = np.arange(BLK)
    dists, oks, groups = [], [], []
    for i, prev_j, next_j in ((np.arange(HALF), full, full[:HALF]), (HALF + np.arange(HALF), full[HALF:], full)):
        i = i[:, None]
        d_prev = BLK + i - prev_j[None]
        d_cur = np.abs(i - full[None])
        d_next = BLK + next_j[None] - i
        tail = np.zeros((HALF, N_META + KEY_PAD), np.int64)
        dists.append(np.concatenate([d_prev, d_cur, d_next, tail], axis=1))
        ok_tail = np.concatenate([np.ones((HALF, N_META), bool), np.zeros((HALF, KEY_PAD), bool)], axis=1)
        oks.append(np.concatenate([d_prev <= BLK, d_cur <= BLK, d_next <= BLK, ok_tail], axis=1))
        groups.append(np.concatenate([np.full(len(prev_j), 0), np.full(BLK, 1), np.full(len(next_j), 2),
                                      np.full(N_META + KEY_PAD, 3)]))
    dist = np.concatenate(dists, axis=0).astype(np.float32)
    ok = np.concatenate(oks, axis=0)
    group = np.stack(groups, axis=0).repeat(HALF, axis=0)
    flags = np.arange(8)[:, None, None]
    group_ok = np.where(group[None] == 3, True, ((flags >> np.minimum(group[None], 2)) & 1) > 0)
    valid = jnp.asarray(group_ok & ok[None])
    base = -(jnp.asarray(slopes)[:, None, None] * jnp.asarray(dist)[None]) * LOG2_E
    return jnp.where(valid[:, None], base[None], NEG_INF).astype(F32)


def _attn_kernel(prev_t, next_t, meta_t, flag_t, q_ref, kp_ref, kc_ref, kn_ref, km_ref,
                 t_ref, sink_ref, gain_ref, o_ref, acc_ref, kk_ref, vv_ref):
    del prev_t, next_t, meta_t
    lo = lax.broadcasted_iota(jnp.int32, (N_KEYS, BLK), 1) < HEAD_DIM
    lo_q = lax.broadcasted_iota(jnp.int32, (HALF, BLK), 1) < HEAD_DIM
    zero = jnp.zeros((), BF16)
    pad = jnp.zeros((KEY_PAD, BLK), BF16)
    ones_lo = jnp.where(lo, 1.0, 0.0).astype(BF16)
    for kh in range(ATTN_KV_HEADS):
        for half in range(2):
            vv_ref[kh, half, :N_KEYS, BLK:] = ones_lo
            vv_ref[kh, half, N_KEYS:, BLK:] = 1.0 - ones_lo

    def stack_keys(kh):
        kcols = slice(kh * BLK, (kh + 1) * BLK)
        vcols = slice(KK_WIDTH + kh * BLK, KK_WIDTH + (kh + 1) * BLK)
        for half in range(2):
            def keys(c):
                if half == 0:
                    parts = [kp_ref[:, c], kc_ref[:, c], kn_ref[:HALF, c]]
                else:
                    parts = [kp_ref[HALF:, c], kc_ref[:, c], kn_ref[:, c]]
                return jnp.concatenate(parts + [km_ref[PAD_ROWS:, c], pad], axis=0)

            kcat, vcat = keys(kcols), keys(vcols)
            kk_ref[kh, half, :N_KEYS, :] = jnp.where(lo, kcat, zero)
            kk_ref[kh, half, N_KEYS:, :] = jnp.where(lo, zero, kcat)
            vv_ref[kh, half, :N_KEYS, :BLK] = jnp.where(lo, vcat, zero)
            vv_ref[kh, half, N_KEYS:, :BLK] = jnp.where(lo, zero, vcat)

    def units(kh):
        return [(kh, half) for half in range(2)]

    def scores(kh, half):
        rows = slice(half * HALF, (half + 1) * HALF)
        q2 = jnp.concatenate([q_ref[rows, (2 * kh + j) * BLK:(2 * kh + j + 1) * BLK] for j in range(2)], axis=0)
        return _dot_nt(q2, kk_ref[kh, half])

    def softmax(s, kh, half):
        rows = slice(half * HALF, (half + 1) * HALF)
        ps, sinks = [], []
        for j in range(2):
            h_e = 4 * kh + 2 * j
            sj = s[j * HALF:(j + 1) * HALF, :]
            s_e = sj[:, :N_KEYS] + t_ref[0, h_e, rows, :]
            s_o = sj[:, N_KEYS:] + t_ref[0, h_e + 1, rows, :]
            m_e = jnp.maximum(jnp.max(s_e, axis=-1, keepdims=True), sink_ref[h_e])
            m_o = jnp.maximum(jnp.max(s_o, axis=-1, keepdims=True), sink_ref[h_e + 1])
            ps.append(jnp.concatenate([jnp.exp2(s_e - m_e), jnp.exp2(s_o - m_o)], axis=1).astype(BF16))
            sinks.append(jnp.where(lo_q, jnp.exp2(sink_ref[h_e] - m_e), jnp.exp2(sink_ref[h_e + 1] - m_o)))
        return jnp.concatenate(ps, axis=0), sinks

    def weighted_values(p, sinks, kh, half):
        rows = slice(half * HALF, (half + 1) * HALF)
        o4 = _dot(p, vv_ref[kh, half])
        for j in range(2):
            oj = o4[j * HALF:(j + 1) * HALF, :]
            col = (2 * kh + j) * BLK
            acc_ref[rows, col:col + BLK] = oj[:, :BLK] / (oj[:, BLK:] + sinks[j])

    for kh in range(ATTN_KV_HEADS):
        stack_keys(kh)
    pending = None
    for kh in range(ATTN_KV_HEADS + 1):
        s_now = [scores(*u) for u in units(kh)] if kh < ATTN_KV_HEADS else None
        if pending is not None:
            probs = [softmax(s, *u) for s, u in zip(pending, units(kh - 1))]
            for (p, sinks), u in zip(probs, units(kh - 1)):
                weighted_values(p, sinks, *u)
        pending = s_now

    o = acc_ref[...]
    ms = jnp.mean(o * o, axis=-1, keepdims=True)
    o_ref[...] = (o * lax.rsqrt(ms + EPS) * gain_ref[...]).astype(BF16)


def _attention(qkv, tables, bias, sink, gain):
    _, prev, nxt, meta, flags = tables
    n_blocks = qkv.shape[0] // BLK
    kv_spec = lambda f: pl.BlockSpec((BLK, 2 * KK_WIDTH), f)
    grid_spec = pltpu.PrefetchScalarGridSpec(
        num_scalar_prefetch=4,
        grid=(n_blocks,),
        in_specs=[
            pl.BlockSpec((BLK, ATTN_WIDTH), lambda i, p, n, m, f: (i, 0)),
            kv_spec(lambda i, p, n, m, f: (p[i], 1)),
            kv_spec(lambda i, p, n, m, f: (i, 1)),
            kv_spec(lambda i, p, n, m, f: (n[i], 1)),
            kv_spec(lambda i, p, n, m, f: (m[i], 1)),
            pl.BlockSpec((1, ATTN_HEADS, BLK, N_KEYS), lambda i, p, n, m, f: (f[i], 0, 0, 0)),
            pl.BlockSpec(memory_space=pltpu.SMEM),
            pl.BlockSpec((1, ATTN_WIDTH), lambda i, p, n, m, f: (0, 0)),
        ],
        out_specs=pl.BlockSpec((BLK, ATTN_WIDTH), lambda i, p, n, m, f: (i, 0)),
        scratch_shapes=[
            pltpu.VMEM((BLK, ATTN_WIDTH), F32),
            pltpu.VMEM((ATTN_KV_HEADS, 2, 2 * N_KEYS, BLK), BF16),
            pltpu.VMEM((ATTN_KV_HEADS, 2, 2 * N_KEYS, 2 * BLK), BF16),
        ],
    )
    return pl.pallas_call(
        _attn_kernel,
        grid_spec=grid_spec,
        out_shape=jax.ShapeDtypeStruct((qkv.shape[0], ATTN_WIDTH), BF16),
        compiler_params=pltpu.CompilerParams(
            dimension_semantics=("arbitrary",), vmem_limit_bytes=VMEM_LIMIT),
        name="attention",
    )(jnp.asarray(prev), jnp.asarray(nxt), jnp.asarray(meta), jnp.asarray(flags),
      qkv, qkv, qkv, qkv, qkv, bias, sink, gain)


_LEVELS = (64, 32, 16, 8, 4, 2, 1)


def _hgrn_consts():
    t = np.arange(BLK)[:, None]
    s = np.arange(BLK)[None, :]
    lmat = (s <= t).astype(np.float32)
    umat = (s >= t).astype(np.float32)
    x = t ^ s
    lv = np.where(x == 0, -1, np.floor(np.log2(np.maximum(x, 1)))).astype(np.int32)
    return jnp.asarray(lmat, BF16), jnp.asarray(umat, BF16), jnp.asarray(lv)


def _level_reference(ac_ref, c, rev):
    off = c - 1 + (1 if rev else 0)
    if c == 4:
        pieces = [jnp.broadcast_to(ac_ref[bs + off:bs + off + 1, :], (8, HG_DIM)) for bs in range(0, BLK, 8)]
        return jnp.concatenate(pieces, axis=0)
    assert c == 2
    sub = lax.broadcasted_iota(jnp.int32, (8, HG_DIM), 0)
    pieces = []
    for bs in range(0, BLK, 8):
        r1 = jnp.broadcast_to(ac_ref[bs + off:bs + off + 1, :], (8, HG_DIM))
        r2 = jnp.broadcast_to(ac_ref[bs + 4 + off:bs + 4 + off + 1, :], (8, HG_DIM))
        pieces.append(jnp.where(sub < 4, r1, r2))
    return jnp.concatenate(pieces, axis=0)


def _upper_level_operands(q, k, ac, ac_ref, c, rev):
    zeros = jnp.zeros((c, HG_DIM), F32)
    q_parts, k_parts = [], []
    for bs in range(0, BLK, 2 * c):
        lo_rows, hi_rows = slice(bs, bs + c), slice(bs + c, bs + 2 * c)
        q_rows, k_rows = (lo_rows, hi_rows) if rev else (hi_rows, lo_rows)
        r = bs + c if rev else bs + c - 1
        ref = ac_ref[r:r + 1, :]
        qb = q[q_rows] * jnp.exp2(ac[q_rows] - ref)
        kb = k[k_rows] * jnp.exp2(ref - ac[k_rows])
        q_parts += [qb, zeros] if rev else [zeros, qb]
        k_parts += [zeros, kb] if rev else [kb, zeros]
    return jnp.concatenate(q_parts, axis=0).astype(BF16), jnp.concatenate(k_parts, axis=0).astype(BF16)


def _hgrn_chunk(qs, zs, vs, lbps, cmat, lv, ac_refs, st_refs, rev):
    n = len(qs)
    off = 2 if rev else 0
    row = lax.broadcasted_iota(jnp.int32, (BLK, HG_DIM), 0)

    fs, ks, acs = [], [], []
    for z, lbp, ac_ref in zip(zs, lbps, ac_refs):
        log_lb, log_1m_lb = lbp[off:off + 1, :], lbp[off + 1:off + 2, :]
        log_sig = jnp.minimum(z, 0.0) - jnp.log(1.0 + jnp.exp(-jnp.abs(z)))
        bv = log_1m_lb + log_sig
        lf = jnp.maximum(log_lb, bv) + jnp.log(1.0 + jnp.exp(-jnp.abs(log_lb - bv)))
        lf = lf * LOG2_E
        f = jnp.exp2(lf)
        k = 1.0 - f
        hi = lf.astype(BF16)
        lo = (lf - hi.astype(F32)).astype(BF16)
        ac2 = _dot(cmat, jnp.concatenate([hi, lo], axis=1))
        ac = ac2[:, :HG_DIM] + ac2[:, HG_DIM:]
        ac_ref[...] = ac
        fs.append(f)
        ks.append(k)
        acs.append(ac)

    ps = [None] * n
    for c in _LEVELS:
        q_rows = ((row & c) == 0) if rev else ((row & c) != 0)
        sign = jnp.where(q_rows, 1.0, -1.0)
        in_level = lv == int(np.log2(c))
        for i in range(n):
            q, k, ac = qs[i], ks[i], acs[i]
            if c >= 8:
                qt, kt = _upper_level_operands(q, k, ac, ac_refs[i], c, rev)
            else:
                if c == 1:
                    w = jnp.where(q_rows, fs[i], 1.0)
                else:
                    w = jnp.exp2((ac - _level_reference(ac_refs[i], c, rev)) * sign)
                qt = jnp.where(q_rows, q * w, 0.0).astype(BF16)
                kt = jnp.where(q_rows, 0.0, k * w).astype(BF16)
            pc = _dot_nt(qt, kt).astype(BF16)
            ps[i] = pc if ps[i] is None else jnp.where(in_level, pc, ps[i])

    outs = []
    for i in range(n):
        diag = jnp.sum(qs[i] * ks[i], axis=-1, keepdims=True).astype(BF16)
        p = jnp.where(lv == -1, diag, ps[i])
        st = st_refs[i][...]
        outs.append(_dot(p, vs[i]) + _dot_nt((qs[i] * jnp.exp2(acs[i])).astype(BF16), st.astype(BF16)))
    for i in range(n):
        edge = ac_refs[i][0:1, :] if rev else ac_refs[i][BLK - 1:BLK, :]
        ke = (ks[i] * jnp.exp2(edge - acs[i])).astype(BF16)
        st_refs[i][...] = st_refs[i][...] * jnp.exp2(edge) + _dot(vs[i].T, ke)
    return outs


def _hgrn_kernel(q_ref, zf_ref, zb_ref, v_ref, g_ref, lbp_ref, gn_ref, lmat_ref, umat_ref, lv_ref,
                 alias_ref, o_ref, st_ref, obuf_ref, ac_ref, *, nrb, tb):
    del alias_ref
    ph = pl.program_id(2)
    rb = pl.program_id(3)
    nch = tb // BLK
    lv = lv_ref[...]

    @pl.when(rb == 0)
    def _():
        st_ref[...] = jnp.zeros_like(st_ref)

    def sweep(rev):
        z_ref = zb_ref if rev else zf_ref
        cmat = umat_ref[...] if rev else lmat_ref[...]
        rb_seq = (nrb - 1 - rb) if rev else rb

        def chunk(ci, parity):
            cj = (nch - 1 - ci) if rev else ci
            r0 = pl.multiple_of(cj * BLK, BLK)
            rows = pl.ds(r0, BLK)
            seq_rows = pl.ds(pl.multiple_of(rb_seq * tb + r0, BLK), BLK)
            heads = range(HG_PER_STEP)
            col = [slice(hh * HG_DIM, (hh + 1) * HG_DIM) for hh in heads]
            outs = _hgrn_chunk([q_ref[rows, c].astype(F32) for c in col],
                               [z_ref[rows, c].astype(F32) for c in col],
                               [v_ref[rows, c] for c in col],
                               [lbp_ref.at[:, c] for c in col], cmat, lv,
                               [ac_ref.at[parity, hh] for hh in heads], [st_ref.at[hh] for hh in heads], rev)
            for cols, o in zip(col, outs):
                if rev:
                    obuf_ref[seq_rows, cols] = o
                else:
                    o = o + obuf_ref[seq_rows, cols]
                    ms = jnp.mean(o * o, axis=-1, keepdims=True)
                    y = o * lax.rsqrt(ms + EPS) * gn_ref[:, cols]
                    g = g_ref[rows, cols].astype(F32)
                    o_ref[rows, cols] = (y * (g / (1.0 + jnp.exp(-g)))).astype(BF16)

        def body(bi, carry):
            for u in range(HG_CHUNKS_PER_BODY):
                chunk(HG_CHUNKS_PER_BODY * bi + u, u)
            return carry

        n_bodies = nch // HG_CHUNKS_PER_BODY
        lax.fori_loop(0, n_bodies, body, 0)
        for u in range(nch % HG_CHUNKS_PER_BODY):
            chunk(n_bodies * HG_CHUNKS_PER_BODY + u, u)

    @pl.when(ph == 0)
    def _():
        sweep(True)

    @pl.when(ph == 1)
    def _():
        sweep(False)


def _hgrn_group(hg, rec, lbp, gn, consts, *, base_blk, batch, nb_seq):
    if nb_seq * BLK <= HG_MAX_SEQ_ROWS:
        kmax = nb_seq
    else:
        kmax = max(k for k in range(1, HG_MAX_BLOCKS_PER_STEP + 1) if nb_seq % k == 0)
    tb = kmax * BLK
    nrb = nb_seq // kmax
    assert (base_blk * BLK) % tb == 0
    base = base_blk * BLK // tb
    lmat, umat, lv = consts

    def rows_in(b, h, ph, rb):
        return base + b * nrb + jnp.where(ph == 0, nrb - 1 - rb, rb)

    def rows_fwd_only(b, h, ph, rb):
        return base + b * nrb + jnp.where(ph == 0, 0, rb)

    def rows_bwd_only(b, h, ph, rb):
        return base + b * nrb + jnp.where(ph == 0, nrb - 1 - rb, 0)

    width = HG_PER_STEP * HG_DIM
    groups = HG_HEADS // HG_PER_STEP
    col_base = QKV_COLS // width
    blk = lambda rows_fn, col0: pl.BlockSpec(
        (tb, width), lambda b, h, ph, rb: (rows_fn(b, h, ph, rb), col0 + h))
    const = lambda shape: pl.BlockSpec(shape, lambda b, h, ph, rb: (0, 0))
    seq_rows = nb_seq * BLK
    return pl.pallas_call(
        functools.partial(_hgrn_kernel, nrb=nrb, tb=tb),
        grid=(batch, groups, 2, nrb),
        in_specs=[
            blk(rows_in, col_base),
            blk(rows_fwd_only, col_base + groups),
            blk(rows_bwd_only, col_base + 2 * groups),
            blk(rows_in, col_base + 3 * groups),
            blk(rows_fwd_only, col_base + 4 * groups),
            pl.BlockSpec((8, width), lambda b, h, ph, rb: (0, h)),
            pl.BlockSpec((1, width), lambda b, h, ph, rb: (0, h)),
            const((BLK, BLK)), const((BLK, BLK)), const((BLK, BLK)),
            pl.BlockSpec(memory_space=pl.ANY),
        ],
        out_specs=blk(rows_fwd_only, 0),
        out_shape=jax.ShapeDtypeStruct(rec.shape, rec.dtype),
        input_output_aliases={10: 0},
        scratch_shapes=[
            pltpu.VMEM((HG_PER_STEP, HG_DIM, HG_DIM), F32),
            pltpu.VMEM((seq_rows, width), F32),
            pltpu.VMEM((HG_CHUNKS_PER_BODY, HG_PER_STEP, BLK, HG_DIM), F32),
        ],
        compiler_params=pltpu.CompilerParams(
            dimension_semantics=("arbitrary",) * 4, vmem_limit_bytes=VMEM_LIMIT),
        name="hgrn",
    )(hg, hg, hg, hg, hg, lbp, gn, lmat, umat, lv, rec)


def _store_valid_rows(o_ref, y, thr_ref, tile, rows_per_tile):
    row = lax.broadcasted_iota(jnp.int32, (BLK, y.shape[1]), 0)
    for kb in range(rows_per_tile // BLK):
        t = thr_ref[tile * (rows_per_tile // BLK) + kb]
        o_ref[kb * BLK:(kb + 1) * BLK, :] = jnp.where(row >= t, y[kb * BLK:(kb + 1) * BLK, :], 0.0)


def _outproj_kernel(thr_ref, a_ref, r_ref, x_ref, wa_ref, wr_ref, o_ref):
    y = x_ref[...] + _dot(a_ref[...], wa_ref[...]) + _dot(r_ref[...], wr_ref[...])
    _store_valid_rows(o_ref, y, thr_ref, pl.program_id(0), TM_RES)


def _outproj(thr, attn, rec, x, wa, wr):
    rows = x.shape[0]
    grid_spec = pltpu.PrefetchScalarGridSpec(
        num_scalar_prefetch=1,
        grid=(rows // TM_RES,),
        in_specs=[
            pl.BlockSpec((TM_RES, ATTN_WIDTH), lambda i, t: (i, 0)),
            pl.BlockSpec((TM_RES, HG_WIDTH), lambda i, t: (i, 0)),
            pl.BlockSpec((TM_RES, D_MODEL), lambda i, t: (i, 0)),
            pl.BlockSpec((ATTN_WIDTH, D_MODEL), lambda i, t: (0, 0)),
            pl.BlockSpec((HG_WIDTH, D_MODEL), lambda i, t: (0, 0)),
        ],
        out_specs=pl.BlockSpec((TM_RES, D_MODEL), lambda i, t: (i, 0)),
    )
    return pl.pallas_call(
        _outproj_kernel,
        grid_spec=grid_spec,
        out_shape=jax.ShapeDtypeStruct((rows, D_MODEL), F32),
        compiler_params=pltpu.CompilerParams(
            dimension_semantics=("arbitrary",), vmem_limit_bytes=VMEM_LIMIT),
        name="outproj",
    )(thr, attn, rec, x, wa, wr)


def _ffn_kernel(thr_ref, x_ref, xp_ref, xn_ref, g_ref, wa_ref, wb_ref, cwa_ref, cwb_ref, cba_ref, cbb_ref,
                wd_ref, gf_ref, o_ref, xs_ref, *, final):
    i = pl.program_id(0)
    f = pl.program_id(1)
    tm = x_ref.shape[0]
    ext = tm + 2 * HALO

    def norm(x):
        ms = jnp.mean(x * x, axis=-1, keepdims=True)
        return (x * lax.rsqrt(ms + EPS) * g_ref[...]).astype(BF16)

    @pl.when(f == 0)
    def _():
        xs_ref[HALO:HALO + tm, :] = norm(x_ref[...])
        xs_ref[0:HALO, :] = norm(xp_ref[...])
        nxt = jnp.where(i == pl.num_programs(0) - 1, 0.0, xn_ref[...])
        xs_ref[HALO + tm:ext, :] = norm(nxt)
        o_ref[...] = jnp.zeros_like(o_ref)

    xs = xs_ref[...]

    def conv(w_ref, cw_ref, cb_ref):
        u = _dot(xs, w_ref[...])
        u_prev = pltpu.roll(u, 1, 0)[HALO:HALO + tm, :]
        u_next = pltpu.roll(u, ext - 1, 0)[HALO:HALO + tm, :]
        cw = cw_ref[...]
        return cw[0:1, :] * u_prev + cw[1:2, :] * u[HALO:HALO + tm, :] + cw[2:3, :] * u_next + cb_ref[...]

    ca = conv(wa_ref, cwa_ref, cba_ref)
    cb = conv(wb_ref, cwb_ref, cbb_ref)
    act = (ca / (1.0 + jnp.exp(-ca))) * cb
    o_ref[...] += _dot(act.astype(BF16), wd_ref[...])

    @pl.when(f == pl.num_programs(1) - 1)
    def _():
        y = x_ref[...] + o_ref[...]
        if final:
            ms = jnp.mean(y * y, axis=-1, keepdims=True)
            y = y * lax.rsqrt(ms + EPS) * gf_ref[...]
        _store_valid_rows(o_ref, y, thr_ref, i, tm)


def _ffn(thr, x, gain, w_up, conv_w, conv_b, w_down, final_gain, *, final):
    rows = x.shape[0]
    nf = D_FF // TF
    per = TM_FFN // HALO
    last_halo = rows // HALO - 1
    grid_spec = pltpu.PrefetchScalarGridSpec(
        num_scalar_prefetch=1,
        grid=(rows // TM_FFN, nf),
        in_specs=[
            pl.BlockSpec((TM_FFN, D_MODEL), lambda i, f, t: (i, 0)),
            pl.BlockSpec((HALO, D_MODEL), lambda i, f, t: (jnp.maximum(i * per - 1, 0), 0)),
            pl.BlockSpec((HALO, D_MODEL), lambda i, f, t: (jnp.minimum((i + 1) * per, last_halo), 0)),
            pl.BlockSpec((1, D_MODEL), lambda i, f, t: (0, 0)),
            pl.BlockSpec((D_MODEL, TF), lambda i, f, t: (0, f)),
            pl.BlockSpec((D_MODEL, TF), lambda i, f, t: (0, nf + f)),
            pl.BlockSpec((3, TF), lambda i, f, t: (0, f)),
            pl.BlockSpec((3, TF), lambda i, f, t: (0, nf + f)),
            pl.BlockSpec((1, TF), lambda i, f, t: (0, f)),
            pl.BlockSpec((1, TF), lambda i, f, t: (0, nf + f)),
            pl.BlockSpec((TF, D_MODEL), lambda i, f, t: (f, 0)),
            pl.BlockSpec((1, D_MODEL), lambda i, f, t: (0, 0)),
        ],
        out_specs=pl.BlockSpec((TM_FFN, D_MODEL), lambda i, f, t: (i, 0)),
        scratch_shapes=[pltpu.VMEM((TM_FFN + 2 * HALO, D_MODEL), BF16)],
    )
    return pl.pallas_call(
        functools.partial(_ffn_kernel, final=final),
        grid_spec=grid_spec,
        out_shape=jax.ShapeDtypeStruct((rows, D_MODEL), F32),
        compiler_params=pltpu.CompilerParams(
            dimension_semantics=("arbitrary", "arbitrary"), vmem_limit_bytes=VMEM_LIMIT),
        name="ffn",
    )(thr, x, x, x, gain, w_up, w_up, conv_w, conv_w, conv_b, conv_b, w_down, final_gain)


def _pack_sequences(h, x, meta_tokens, first_blk, nb_seq):
    batch = x.shape[0]
    n_blocks = h.shape[0] // BLK
    meta = meta_tokens.astype(h.dtype)
    if batch > 1 and first_blk % nb_seq == 0 and n_blocks % nb_seq == 0:
        slots = h.reshape(n_blocks // nb_seq, nb_seq, BLK, D_MODEL)
        first = first_blk // nb_seq
        slots = slots.at[first:first + batch, 1:].set(x.reshape(batch, nb_seq - 1, BLK, D_MODEL))
        slots = slots.at[first:first + batch, 0, PAD_ROWS:].set(
            jnp.broadcast_to(meta[None], (batch, N_META, D_MODEL)))
        return slots.reshape(h.shape)
    for b in range(batch):
        r0 = (first_blk + b * nb_seq) * BLK
        h = h.at[r0 + PAD_ROWS:r0 + BLK].set(meta)
        h = h.at[r0 + BLK:r0 + nb_seq * BLK].set(x[b])
    return h


def _unpack_sequences(h, first_blk, batch, nb_seq):
    n_blocks = h.shape[0] // BLK
    if batch == 1:
        return h[(first_blk + 1) * BLK:(first_blk + nb_seq) * BLK][None]
    if first_blk % nb_seq == 0 and n_blocks % nb_seq == 0:
        slots = h.reshape(n_blocks // nb_seq, nb_seq, BLK, D_MODEL)
        first = first_blk // nb_seq
        return slots[first:first + batch, 1:].reshape(batch, (nb_seq - 1) * BLK, D_MODEL)
    rows = h[first_blk * BLK:(first_blk + batch * nb_seq) * BLK]
    return rows.reshape(batch, nb_seq * BLK, D_MODEL)[:, BLK:]


def _in_weights(w):
    q = w[:, :ATTN_WIDTH] * (HEAD_DIM ** -0.5 * LOG2_E)
    k = w[:, ATTN_WIDTH:ATTN_WIDTH + KV_WIDTH].reshape(D_MODEL, ATTN_KV_HEADS, 1, HEAD_DIM)
    v = w[:, ATTN_WIDTH + KV_WIDTH:ATTN_WIDTH + 2 * KV_WIDTH].reshape(D_MODEL, ATTN_KV_HEADS, 1, HEAD_DIM)
    kk = jnp.broadcast_to(k, (D_MODEL, ATTN_KV_HEADS, 2, HEAD_DIM)).reshape(D_MODEL, KK_WIDTH)
    vv = jnp.broadcast_to(v, (D_MODEL, ATTN_KV_HEADS, 2, HEAD_DIM)).reshape(D_MODEL, KK_WIDTH)
    hg = w[:, ATTN_WIDTH + 2 * KV_WIDTH:]
    return jnp.concatenate([q, kk, vv, hg], axis=1).astype(BF16)


def _lower_bound_params(lb_param, layer):
    c = jnp.cumsum(jax.nn.softmax(lb_param.astype(F32), axis=1), axis=1)
    lb = (c - c[:, :1])[:, layer]
    rows = [jnp.log(lb[0]), jnp.log1p(-lb[0]), jnp.log(lb[1]), jnp.log1p(-lb[1])]
    rows += [jnp.zeros_like(lb[0])] * 4
    return jnp.stack(rows, axis=0)


def kernel(x_prompt, x_sample, meta_tokens, mix_norm, w_in, attn_sink, attn_out_norm, hgrn_lower_bounds,
           hgrn_out_norm, w_out, ffn_norm, w_up, conv_w, conv_b, w_down, final_norm):
    bp, sp, _ = x_prompt.shape
    bs, ss, _ = x_sample.shape
    depth = w_in.shape[0]
    lay = _make_layout(bp, sp, bs, ss)
    tables = _block_tables(lay)
    thr = jnp.asarray(tables[0])
    bias = _attn_bias_tables()
    consts = _hgrn_consts()

    rows = lay.n_blocks * BLK
    h = jnp.zeros((rows, D_MODEL), F32)
    h = _pack_sequences(h, x_sample, meta_tokens, 0, lay.nb_s)
    h = _pack_sequences(h, x_prompt, meta_tokens, lay.base_p, lay.nb_p)

    row2 = lambda t: t.reshape(1, -1).astype(F32)
    for l in range(depth):
        proj = _inproj(h, row2(mix_norm[l]), _in_weights(w_in[l]))
        attn = _attention(proj, tables, bias, attn_sink[l].astype(F32) * LOG2_E, row2(attn_out_norm[l]))
        lbp = _lower_bound_params(hgrn_lower_bounds, l)
        gn = row2(hgrn_out_norm[l])
        rec = jnp.zeros((rows, HG_WIDTH), BF16)
        rec = _hgrn_group(proj, rec, lbp, gn, consts, base_blk=0, batch=bs, nb_seq=lay.nb_s)
        rec = _hgrn_group(proj, rec, lbp, gn, consts, base_blk=lay.base_p, batch=bp, nb_seq=lay.nb_p)
        wo = w_out[l].astype(BF16)
        h = _outproj(thr, attn, rec, h, wo[:ATTN_WIDTH], wo[ATTN_WIDTH:])
        h = _ffn(thr, h, row2(ffn_norm[l]), w_up[l].astype(BF16), conv_w[l].astype(F32),
                 row2(conv_b[l]), w_down[l].astype(BF16), row2(final_norm), final=(l == depth - 1))

    return (_unpack_sequences(h, lay.base_p, bp, lay.nb_p), _unpack_sequences(h, 0, bs, lay.nb_s))
```

```python
import functools
from typing import NamedTuple

import numpy as np
import jax
import jax.numpy as jnp
from jax import lax
from jax.experimental import pallas as pl
from jax.experimental.pallas import tpu as pltpu

F32 = jnp.float32
BF16 = jnp.bfloat16

D_MODEL = 2048
N_META = 16
HEAD_DIM = 64
ATTN_WIDTH = 1024
ATTN_HEADS = 16
ATTN_KV_HEADS = 4
KV_WIDTH = ATTN_KV_HEADS * HEAD_DIM
BLK = 128
PAD_ROWS = BLK - N_META
HG_HEADS = 8
HG_DIM = 128
HG_WIDTH = HG_HEADS * HG_DIM
HG_PER_STEP = 4
HG_CHUNKS_PER_BODY = 3
HG_MAX_SEQ_ROWS = 4096
HG_MAX_BLOCKS_PER_STEP = 4
D_FF = 5632
EPS = 1e-6
NEG_INF = -1e30
LOG2_E = 1.4426950408889634
KK_WIDTH = ATTN_KV_HEADS * 128
QKV_COLS = ATTN_WIDTH + 2 * KK_WIDTH
IN_COLS2 = QKV_COLS + 5 * HG_WIDTH

TM = 1024
TM_RES = 512
TM_FFN = 768
TN_IN = 1792
TF = 512
HALO = 8
VMEM_LIMIT = 56 * 1024 * 1024


def _dot(a, b):
    return jnp.dot(a, b, preferred_element_type=F32)


def _dot_nt(a, b):
    return lax.dot_general(a, b, (((1,), (1,)), ((), ())), preferred_element_type=F32)


class _Layout(NamedTuple):
    nb_s: int
    nb_p: int
    base_p: int
    n_blocks: int
    seqs: tuple


def _make_layout(bp, sp, bs, ss):
    nb_s, nb_p = ss // BLK + 1, sp // BLK + 1
    base_p = -(-(bs * nb_s) // nb_p) * nb_p
    used = base_p + bp * nb_p
    per_tile = int(np.lcm.reduce([TM // BLK, TM_RES // BLK, TM_FFN // BLK]))
    n_blocks = -(-used // per_tile) * per_tile
    seqs = tuple((b * nb_s, nb_s) for b in range(bs)) + tuple((base_p + b * nb_p, nb_p) for b in range(bp))
    return _Layout(nb_s, nb_p, base_p, n_blocks, seqs)


def _block_tables(lay):
    n = lay.n_blocks
    thr = np.full((n,), BLK, np.int32)
    prev = np.arange(n, dtype=np.int32)
    nxt = np.arange(n, dtype=np.int32)
    meta = np.arange(n, dtype=np.int32)
    flags = np.zeros((n,), np.int32)
    for first, cnt in lay.seqs:
        for i in range(cnt):
            blk = first + i
            thr[blk] = PAD_ROWS if i == 0 else 0
            prev[blk] = max(blk - 1, first)
            nxt[blk] = min(blk + 1, first + cnt - 1)
            meta[blk] = first
            flags[blk] = (1 if i >= 2 else 0) | (2 if i >= 1 else 0) | (4 if i + 1 <= cnt - 1 else 0)
    return thr, prev, nxt, meta, flags


def _inproj_kernel(x_ref, g_ref, w_ref, o_ref, xn_ref):
    @pl.when(pl.program_id(1) == 0)
    def _():
        x = x_ref[...]
        ms = jnp.mean(x * x, axis=-1, keepdims=True)
        xn_ref[...] = (x * lax.rsqrt(ms + EPS) * g_ref[...]).astype(BF16)

    o_ref[...] = _dot(xn_ref[...], w_ref[...]).astype(BF16)


def _inproj(x, gain, w):
    rows = x.shape[0]
    return pl.pallas_call(
        _inproj_kernel,
        grid=(rows // TM, IN_COLS2 // TN_IN),
        in_specs=[
            pl.BlockSpec((TM, D_MODEL), lambda i, j: (i, 0)),
            pl.BlockSpec((1, D_MODEL), lambda i, j: (0, 0)),
            pl.BlockSpec((D_MODEL, TN_IN), lambda i, j: (0, j)),
        ],
        out_specs=pl.BlockSpec((TM, TN_IN), lambda i, j: (i, j)),
        out_shape=jax.ShapeDtypeStruct((rows, IN_COLS2), BF16),
        scratch_shapes=[pltpu.VMEM((TM, D_MODEL), BF16)],
        compiler_params=pltpu.CompilerParams(
            dimension_semantics=("arbitrary", "arbitrary"), vmem_limit_bytes=VMEM_LIMIT),
        name="inproj",
    )(x, gain, w)


HALF = BLK // 2
N_KEYS = 3 * BLK
KEY_PAD = N_KEYS - (2 * BLK + HALF + N_META)


def _attn_bias_tables():
    slopes = (2.0 ** (-8.0 * np.arange(1, ATTN_HEADS + 1) / ATTN_HEADS)).astype(np.float32)
    full = np.arange(BLK)
    dists, oks, groups = [], [], []
    for i, prev_j, next_j in ((np.arange(HALF), full, full[:HALF]), (HALF + np.arange(HALF), full[HALF:], full)):
        i = i[:, None]
        d_prev = BLK + i - prev_j[None]
        d_cur = np.abs(i - full[None])
        d_next = BLK + next_j[None] - i
        tail = np.zeros((HALF, N_META + KEY_PAD), np.int64)
        dists.append(np.concatenate([d_prev, d_cur, d_next, tail], axis=1))
        ok_tail = np.concatenate([np.ones((HALF, N_META), bool), np.zeros((HALF, KEY_PAD), bool)], axis=1)
        oks.append(np.concatenate([d_prev <= BLK, d_cur <= BLK, d_next <= BLK, ok_tail], axis=1))
        groups.append(np.concatenate([np.full(len(prev_j), 0), np.full(BLK, 1), np.full(len(next_j), 2),
                                      np.full(N_META + KEY_PAD, 3)]))
    dist = np.concatenate(dists, axis=0).astype(np.float32)
    ok = np.concatenate(oks, axis=0)
    group = np.stack(groups, axis=0).repeat(HALF, axis=0)
    flags = np.arange(8)[:, None, None]
    group_ok = np.where(group[None] == 3, True, ((flags >> np.minimum(group[None], 2)) & 1) > 0)
    valid = jnp.asarray(group_ok & ok[None])
    base = -(jnp.asarray(slopes)[:, None, None] * jnp.asarray(dist)[None]) * LOG2_E
    return jnp.where(valid[:, None], base[None], NEG_INF).astype(F32)


def _attn_kernel(prev_t, next_t, meta_t, flag_t, q_ref, kp_ref, kc_ref, kn_ref, km_ref,
                 t_ref, sink_ref, gain_ref, o_ref, acc_ref, kk_ref, vv_ref):
    del prev_t, next_t, meta_t
    lo = lax.broadcasted_iota(jnp.int32, (N_KEYS, BLK), 1) < HEAD_DIM
    lo_q = lax.broadcasted_iota(jnp.int32, (HALF, BLK), 1) < HEAD_DIM
    zero = jnp.zeros((), BF16)
    pad = jnp.zeros((KEY_PAD, BLK), BF16)
    ones_lo = jnp.where(lo, 1.0, 0.0).astype(BF16)
    for kh in range(ATTN_KV_HEADS):
        for half in range(2):
            vv_ref[kh, half, :N_KEYS, BLK:] = ones_lo
            vv_ref[kh, half, N_KEYS:, BLK:] = 1.0 - ones_lo

    def stack_keys(kh):
        kcols = slice(kh * BLK, (kh + 1) * BLK)
        vcols = slice(KK_WIDTH + kh * BLK, KK_WIDTH + (kh + 1) * BLK)
        for half in range(2):
            def keys(c):
                if half == 0:
                    parts = [kp_ref[:, c], kc_ref[:, c], kn_ref[:HALF, c]]
                else:
                    parts = [kp_ref[HALF:, c], kc_ref[:, c], kn_ref[:, c]]
                return jnp.concatenate(parts + [km_ref[PAD_ROWS:, c], pad], axis=0)

            kcat, vcat = keys(kcols), keys(vcols)
            kk_ref[kh, half, :N_KEYS, :] = jnp.where(lo, kcat, zero)
            kk_ref[kh, half, N_KEYS:, :] = jnp.where(lo, zero, kcat)
            vv_ref[kh, half, :N_KEYS, :BLK] = jnp.where(lo, vcat, zero)
            vv_ref[kh, half, N_KEYS:, :BLK] = jnp.where(lo, zero, vcat)

    def units(kh):
        return [(kh, half) for half in range(2)]

    def scores(kh, half):
        rows = slice(half * HALF, (half + 1) * HALF)
        q2 = jnp.concatenate([q_ref[rows, (2 * kh + j) * BLK:(2 * kh + j + 1) * BLK] for j in range(2)], axis=0)
        return _dot_nt(q2, kk_ref[kh, half])

    def softmax(s, kh, half):
        rows = slice(half * HALF, (half + 1) * HALF)
        ps, sinks = [], []
        for j in range(2):
            h_e = 4 * kh + 2 * j
            sj = s[j * HALF:(j + 1) * HALF, :]
            s_e = sj[:, :N_KEYS] + t_ref[0, h_e, rows, :]
            s_o = sj[:, N_KEYS:] + t_ref[0, h_e + 1, rows, :]
            m_e = jnp.maximum(jnp.max(s_e, axis=-1, keepdims=True), sink_ref[h_e])
            m_o = jnp.maximum(jnp.max(s_o, axis=-1, keepdims=True), sink_ref[h_e + 1])
            ps.append(jnp.concatenate([jnp.exp2(s_e - m_e), jnp.exp2(s_o - m_o)], axis=1).astype(BF16))
            sinks.append(jnp.where(lo_q, jnp.exp2(sink_ref[h_e] - m_e), jnp.exp2(sink_ref[h_e + 1] - m_o)))
        return jnp.concatenate(ps, axis=0), sinks

    def weighted_values(p, sinks, kh, half):
        rows = slice(half * HALF, (half + 1) * HALF)
        o4 = _dot(p, vv_ref[kh, half])
        for j in range(2):
            oj = o4[j * HALF:(j + 1) * HALF, :]
            col = (2 * kh + j) * BLK
            acc_ref[rows, col:col + BLK] = oj[:, :BLK] / (oj[:, BLK:] + sinks[j])

    for kh in range(ATTN_KV_HEADS):
        stack_keys(kh)
    pending = None
    for kh in range(ATTN_KV_HEADS + 1):
        s_now = [scores(*u) for u in units(kh)] if kh < ATTN_KV_HEADS else None
        if pending is not None:
            probs = [softmax(s, *u) for s, u in zip(pending, units(kh - 1))]
            for (p, sinks), u in zip(probs, units(kh - 1)):
                weighted_values(p, sinks, *u)
        pending = s_now

    o = acc_ref[...]
    ms = jnp.mean(o * o, axis=-1, keepdims=True)
    o_ref[...] = (o * lax.rsqrt(ms + EPS) * gain_ref[...]).astype(BF16)


def _attention(qkv, tables, bias, sink, gain):
    _, prev, nxt, meta, flags = tables
    n_blocks = qkv.shape[0] // BLK
    kv_spec = lambda f: pl.BlockSpec((BLK, 2 * KK_WIDTH), f)
    grid_spec = pltpu.PrefetchScalarGridSpec(
        num_scalar_prefetch=4,
        grid=(n_blocks,),
        in_specs=[
            pl.BlockSpec((BLK, ATTN_WIDTH), lambda i, p, n, m, f: (i, 0)),
            kv_spec(lambda i, p, n, m, f: (p[i], 1)),
            kv_spec(lambda i, p, n, m, f: (i, 1)),
            kv_spec(lambda i, p, n, m, f: (n[i], 1)),
            kv_spec(lambda i, p, n, m, f: (m[i], 1)),
            pl.BlockSpec((1, ATTN_HEADS, BLK, N_KEYS), lambda i, p, n, m, f: (f[i], 0, 0, 0)),
            pl.BlockSpec(memory_space=pltpu.SMEM),
            pl.BlockSpec((1, ATTN_WIDTH), lambda i, p, n, m, f: (0, 0)),
        ],
        out_specs=pl.BlockSpec((BLK, ATTN_WIDTH), lambda i, p, n, m, f: (i, 0)),
        scratch_shapes=[
            pltpu.VMEM((BLK, ATTN_WIDTH), F32),
            pltpu.VMEM((ATTN_KV_HEADS, 2, 2 * N_KEYS, BLK), BF16),
            pltpu.VMEM((ATTN_KV_HEADS, 2, 2 * N_KEYS, 2 * BLK), BF16),
        ],
    )
    return pl.pallas_call(
        _attn_kernel,
        grid_spec=grid_spec,
        out_shape=jax.ShapeDtypeStruct((qkv.shape[0], ATTN_WIDTH), BF16),
        compiler_params=pltpu.CompilerParams(
            dimension_semantics=("arbitrary",), vmem_limit_bytes=VMEM_LIMIT),
        name="attention",
    )(jnp.asarray(prev), jnp.asarray(nxt), jnp.asarray(meta), jnp.asarray(flags),
      qkv, qkv, qkv, qkv, qkv, bias, sink, gain)


_LEVELS = (64, 32, 16, 8, 4, 2, 1)


def _hgrn_consts():
    t = np.arange(BLK)[:, None]
    s = np.arange(BLK)[None, :]
    lmat = (s <= t).astype(np.float32)
    umat = (s >= t).astype(np.float32)
    x = t ^ s
    lv = np.where(x == 0, -1, np.floor(np.log2(np.maximum(x, 1)))).astype(np.int32)
    return jnp.asarray(lmat, BF16), jnp.asarray(umat, BF16), jnp.asarray(lv)


def _level_reference(ac_ref, c, rev):
    off = c - 1 + (1 if rev else 0)
    if c == 4:
        pieces = [jnp.broadcast_to(ac_ref[bs + off:bs + off + 1, :], (8, HG_DIM)) for bs in range(0, BLK, 8)]
        return jnp.concatenate(pieces, axis=0)
    assert c == 2
    sub = lax.broadcasted_iota(jnp.int32, (8, HG_DIM), 0)
    pieces = []
    for bs in range(0, BLK, 8):
        r1 = jnp.broadcast_to(ac_ref[bs + off:bs + off + 1, :], (8, HG_DIM))
        r2 = jnp.broadcast_to(ac_ref[bs + 4 + off:bs + 4 + off + 1, :], (8, HG_DIM))
        pieces.append(jnp.where(sub < 4, r1, r2))
    return jnp.concatenate(pieces, axis=0)


def _upper_level_operands(q, k, ac, ac_ref, c, rev):
    zeros = jnp.zeros((c, HG_DIM), F32)
    q_parts, k_parts = [], []
    for bs in range(0, BLK, 2 * c):
        lo_rows, hi_rows = slice(bs, bs + c), slice(bs + c, bs + 2 * c)
        q_rows, k_rows = (lo_rows, hi_rows) if rev else (hi_rows, lo_rows)
        r = bs + c if rev else bs + c - 1
        ref = ac_ref[r:r + 1, :]
        qb = q[q_rows] * jnp.exp2(ac[q_rows] - ref)
        kb = k[k_rows] * jnp.exp2(ref - ac[k_rows])
        q_parts += [qb, zeros] if rev else [zeros, qb]
        k_parts += [zeros, kb] if rev else [kb, zeros]
    return jnp.concatenate(q_parts, axis=0).astype(BF16), jnp.concatenate(k_parts, axis=0).astype(BF16)


def _hgrn_chunk(qs, zs, vs, lbps, cmat, lv, ac_refs, st_refs, rev):
    n = len(qs)
    off = 2 if rev else 0
    row = lax.broadcasted_iota(jnp.int32, (BLK, HG_DIM), 0)

    fs, ks, acs = [], [], []
    for z, lbp, ac_ref in zip(zs, lbps, ac_refs):
        log_lb, log_1m_lb = lbp[off:off + 1, :], lbp[off + 1:off + 2, :]
        log_sig = jnp.minimum(z, 0.0) - jnp.log(1.0 + jnp.exp(-jnp.abs(z)))
        bv = log_1m_lb + log_sig
        lf = jnp.maximum(log_lb, bv) + jnp.log(1.0 + jnp.exp(-jnp.abs(log_lb - bv)))
        lf = lf * LOG2_E
        f = jnp.exp2(lf)
        k = 1.0 - f
        hi = lf.astype(BF16)
        lo = (lf - hi.astype(F32)).astype(BF16)
        ac2 = _dot(cmat, jnp.concatenate([hi, lo], axis=1))
        ac = ac2[:, :HG_DIM] + ac2[:, HG_DIM:]
        ac_ref[...] = ac
        fs.append(f)
        ks.append(k)
        acs.append(ac)

    ps = [None] * n
    for c in _LEVELS:
        q_rows = ((row & c) == 0) if rev else ((row & c) != 0)
        sign = jnp.where(q_rows, 1.0, -1.0)
        in_level = lv == int(np.log2(c))
        for i in range(n):
            q, k, ac = qs[i], ks[i], acs[i]
            if c >= 8:
                qt, kt = _upper_level_operands(q, k, ac, ac_refs[i], c, rev)
            else:
                if c == 1:
                    w = jnp.where(q_rows, fs[i], 1.0)
                else:
                    w = jnp.exp2((ac - _level_reference(ac_refs[i], c, rev)) * sign)
                qt = jnp.where(q_rows, q * w, 0.0).astype(BF16)
                kt = jnp.where(q_rows, 0.0, k * w).astype(BF16)
            pc = _dot_nt(qt, kt).astype(BF16)
            ps[i] = pc if ps[i] is None else jnp.where(in_level, pc, ps[i])

    outs = []
    for i in range(n):
        diag = jnp.sum(qs[i] * ks[i], axis=-1, keepdims=True).astype(BF16)
        p = jnp.where(lv == -1, diag, ps[i])
        st = st_refs[i][...]
        outs.append(_dot(p, vs[i]) + _dot_nt((qs[i] * jnp.exp2(acs[i])).astype(BF16), st.astype(BF16)))
    for i in range(n):
        edge = ac_refs[i][0:1, :] if rev else ac_refs[i][BLK - 1:BLK, :]
        ke = (ks[i] * jnp.exp2(edge - acs[i])).astype(BF16)
        st_refs[i][...] = st_refs[i][...] * jnp.exp2(edge) + _dot(vs[i].T, ke)
    return outs


def _hgrn_kernel(q_ref, zf_ref, zb_ref, v_ref, g_ref, lbp_ref, gn_ref, lmat_ref, umat_ref, lv_ref,
                 alias_ref, o_ref, st_ref, obuf_ref, ac_ref, *, nrb, tb):
    del alias_ref
    ph = pl.program_id(2)
    rb = pl.program_id(3)
    nch = tb // BLK
    lv = lv_ref[...]

    @pl.when(rb == 0)
    def _():
        st_ref[...] = jnp.zeros_like(st_ref)

    def sweep(rev):
        z_ref = zb_ref if rev else zf_ref
        cmat = umat_ref[...] if rev else lmat_ref[...]
        rb_seq = (nrb - 1 - rb) if rev else rb

        def chunk(ci, parity):
            cj = (nch - 1 - ci) if rev else ci
            r0 = pl.multiple_of(cj * BLK, BLK)
            rows = pl.ds(r0, BLK)
            seq_rows = pl.ds(pl.multiple_of(rb_seq * tb + r0, BLK), BLK)
            heads = range(HG_PER_STEP)
            col = [slice(hh * HG_DIM, (hh + 1) * HG_DIM) for hh in heads]
            outs = _hgrn_chunk([q_ref[rows, c].astype(F32) for c in col],
                               [z_ref[rows, c].astype(F32) for c in col],
                               [v_ref[rows, c] for c in col],
                               [lbp_ref.at[:, c] for c in col], cmat, lv,
                               [ac_ref.at[parity, hh] for hh in heads], [st_ref.at[hh] for hh in heads], rev)
            for cols, o in zip(col, outs):
                if rev:
                    obuf_ref[seq_rows, cols] = o
                else:
                    o = o + obuf_ref[seq_rows, cols]
                    ms = jnp.mean(o * o, axis=-1, keepdims=True)
                    y = o * lax.rsqrt(ms + EPS) * gn_ref[:, cols]
                    g = g_ref[rows, cols].astype(F32)
                    o_ref[rows, cols] = (y * (g / (1.0 + jnp.exp(-g)))).astype(BF16)

        def body(bi, carry):
            for u in range(HG_CHUNKS_PER_BODY):
                chunk(HG_CHUNKS_PER_BODY * bi + u, u)
            return carry

        n_bodies = nch // HG_CHUNKS_PER_BODY
        lax.fori_loop(0, n_bodies, body, 0)
        for u in range(nch % HG_CHUNKS_PER_BODY):
            chunk(n_bodies * HG_CHUNKS_PER_BODY + u, u)

    @pl.when(ph == 0)
    def _():
        sweep(True)

    @pl.when(ph == 1)
    def _():
        sweep(False)


def _hgrn_group(hg, rec, lbp, gn, consts, *, base_blk, batch, nb_seq):
    if nb_seq * BLK <= HG_MAX_SEQ_ROWS:
        kmax = nb_seq
    else:
        kmax = max(k for k in range(1, HG_MAX_BLOCKS_PER_STEP + 1) if nb_seq % k == 0)
    tb = kmax * BLK
    nrb = nb_seq // kmax
    assert (base_blk * BLK) % tb == 0
    base = base_blk * BLK // tb
    lmat, umat, lv = consts

    def rows_in(b, h, ph, rb):
        return base + b * nrb + jnp.where(ph == 0, nrb - 1 - rb, rb)

    def rows_fwd_only(b, h, ph, rb):
        return base + b * nrb + jnp.where(ph == 0, 0, rb)

    def rows_bwd_only(b, h, ph, rb):
        return base + b * nrb + jnp.where(ph == 0, nrb - 1 - rb, 0)

    width = HG_PER_STEP * HG_DIM
    groups = HG_HEADS // HG_PER_STEP
    col_base = QKV_COLS // width
    blk = lambda rows_fn, col0: pl.BlockSpec(
        (tb, width), lambda b, h, ph, rb: (rows_fn(b, h, ph, rb), col0 + h))
    const = lambda shape: pl.BlockSpec(shape, lambda b, h, ph, rb: (0, 0))
    seq_rows = nb_seq * BLK
    return pl.pallas_call(
        functools.partial(_hgrn_kernel, nrb=nrb, tb=tb),
        grid=(batch, groups, 2, nrb),
        in_specs=[
            blk(rows_in, col_base),
            blk(rows_fwd_only, col_base + groups),
            blk(rows_bwd_only, col_base + 2 * groups),
            blk(rows_in, col_base + 3 * groups),
            blk(rows_fwd_only, col_base + 4 * groups),
            pl.BlockSpec((8, width), lambda b, h, ph, rb: (0, h)),
            pl.BlockSpec((1, width), lambda b, h, ph, rb: (0, h)),
            const((BLK, BLK)), const((BLK, BLK)), const((BLK, BLK)),
            pl.BlockSpec(memory_space=pl.ANY),
        ],
        out_specs=blk(rows_fwd_only, 0),
        out_shape=jax.ShapeDtypeStruct(rec.shape, rec.dtype),
        input_output_aliases={10: 0},
        scratch_shapes=[
            pltpu.VMEM((HG_PER_STEP, HG_DIM, HG_DIM), F32),
            pltpu.VMEM((seq_rows, width), F32),
            pltpu.VMEM((HG_CHUNKS_PER_BODY, HG_PER_STEP, BLK, HG_DIM), F32),
        ],
        compiler_params=pltpu.CompilerParams(
            dimension_semantics=("arbitrary",) * 4, vmem_limit_bytes=VMEM_LIMIT),
        name="hgrn",
    )(hg, hg, hg, hg, hg, lbp, gn, lmat, umat, lv, rec)


def _store_valid_rows(o_ref, y, thr_ref, tile, rows_per_tile):
    row = lax.broadcasted_iota(jnp.int32, (BLK, y.shape[1]), 0)
    for kb in range(rows_per_tile // BLK):
        t = thr_ref[tile * (rows_per_tile // BLK) + kb]
        o_ref[kb * BLK:(kb + 1) * BLK, :] = jnp.where(row >= t, y[kb * BLK:(kb + 1) * BLK, :], 0.0)


def _outproj_kernel(thr_ref, a_ref, r_ref, x_ref, wa_ref, wr_ref, o_ref):
    y = x_ref[...] + _dot(a_ref[...], wa_ref[...]) + _dot(r_ref[...], wr_ref[...])
    _store_valid_rows(o_ref, y, thr_ref, pl.program_id(0), TM_RES)


def _outproj(thr, attn, rec, x, wa, wr):
    rows = x.shape[0]
    grid_spec = pltpu.PrefetchScalarGridSpec(
        num_scalar_prefetch=1,
        grid=(rows // TM_RES,),
        in_specs=[
            pl.BlockSpec((TM_RES, ATTN_WIDTH), lambda i, t: (i, 0)),
            pl.BlockSpec((TM_RES, HG_WIDTH), lambda i, t: (i, 0)),
            pl.BlockSpec((TM_RES, D_MODEL), lambda i, t: (i, 0)),
            pl.BlockSpec((ATTN_WIDTH, D_MODEL), lambda i, t: (0, 0)),
            pl.BlockSpec((HG_WIDTH, D_MODEL), lambda i, t: (0, 0)),
        ],
        out_specs=pl.BlockSpec((TM_RES, D_MODEL), lambda i, t: (i, 0)),
    )
    return pl.pallas_call(
        _outproj_kernel,
        grid_spec=grid_spec,
        out_shape=jax.ShapeDtypeStruct((rows, D_MODEL), F32),
        compiler_params=pltpu.CompilerParams(
            dimension_semantics=("arbitrary",), vmem_limit_bytes=VMEM_LIMIT),
        name="outproj",
    )(thr, attn, rec, x, wa, wr)


def _ffn_kernel(thr_ref, x_ref, xp_ref, xn_ref, g_ref, wa_ref, wb_ref, cwa_ref, cwb_ref, cba_ref, cbb_ref,
                wd_ref, gf_ref, o_ref, xs_ref, *, final):
    i = pl.program_id(0)
    f = pl.program_id(1)
    tm = x_ref.shape[0]
    ext = tm + 2 * HALO

    def norm(x):
        ms = jnp.mean(x * x, axis=-1, keepdims=True)
        return (x * lax.rsqrt(ms + EPS) * g_ref[...]).astype(BF16)

    @pl.when(f == 0)
    def _():
        xs_ref[HALO:HALO + tm, :] = norm(x_ref[...])
        xs_ref[0:HALO, :] = norm(xp_ref[...])
        nxt = jnp.where(i == pl.num_programs(0) - 1, 0.0, xn_ref[...])
        xs_ref[HALO + tm:ext, :] = norm(nxt)
        o_ref[...] = jnp.zeros_like(o_ref)

    xs = xs_ref[...]

    def conv(w_ref, cw_ref, cb_ref):
        u = _dot(xs, w_ref[...])
        u_prev = pltpu.roll(u, 1, 0)[HALO:HALO + tm, :]
        u_next = pltpu.roll(u, ext - 1, 0)[HALO:HALO + tm, :]
        cw = cw_ref[...]
        return cw[0:1, :] * u_prev + cw[1:2, :] * u[HALO:HALO + tm, :] + cw[2:3, :] * u_next + cb_ref[...]

    ca = conv(wa_ref, cwa_ref, cba_ref)
    cb = conv(wb_ref, cwb_ref, cbb_ref)
    act = (ca / (1.0 + jnp.exp(-ca))) * cb
    o_ref[...] += _dot(act.astype(BF16), wd_ref[...])

    @pl.when(f == pl.num_programs(1) - 1)
    def _():
        y = x_ref[...] + o_ref[...]
        if final:
            ms = jnp.mean(y * y, axis=-1, keepdims=True)
            y = y * lax.rsqrt(ms + EPS) * gf_ref[...]
        _store_valid_rows(o_ref, y, thr_ref, i, tm)


def _ffn(thr, x, gain, w_up, conv_w, conv_b, w_down, final_gain, *, final):
    rows = x.shape[0]
    nf = D_FF // TF
    per = TM_FFN // HALO
    last_halo = rows // HALO - 1
    grid_spec = pltpu.PrefetchScalarGridSpec(
        num_scalar_prefetch=1,
        grid=(rows // TM_FFN, nf),
        in_specs=[
            pl.BlockSpec((TM_FFN, D_MODEL), lambda i, f, t: (i, 0)),
            pl.BlockSpec((HALO, D_MODEL), lambda i, f, t: (jnp.maximum(i * per - 1, 0), 0)),
            pl.BlockSpec((HALO, D_MODEL), lambda i, f, t: (jnp.minimum((i + 1) * per, last_halo), 0)),
            pl.BlockSpec((1, D_MODEL), lambda i, f, t: (0, 0)),
            pl.BlockSpec((D_MODEL, TF), lambda i, f, t: (0, f)),
            pl.BlockSpec((D_MODEL, TF), lambda i, f, t: (0, nf + f)),
            pl.BlockSpec((3, TF), lambda i, f, t: (0, f)),
            pl.BlockSpec((3, TF), lambda i, f, t: (0, nf + f)),
            pl.BlockSpec((1, TF), lambda i, f, t: (0, f)),
            pl.BlockSpec((1, TF), lambda i, f, t: (0, nf + f)),
            pl.BlockSpec((TF, D_MODEL), lambda i, f, t: (f, 0)),
            pl.BlockSpec((1, D_MODEL), lambda i, f, t: (0, 0)),
        ],
        out_specs=pl.BlockSpec((TM_FFN, D_MODEL), lambda i, f, t: (i, 0)),
        scratch_shapes=[pltpu.VMEM((TM_FFN + 2 * HALO, D_MODEL), BF16)],
    )
    return pl.pallas_call(
        functools.partial(_ffn_kernel, final=final),
        grid_spec=grid_spec,
        out_shape=jax.ShapeDtypeStruct((rows, D_MODEL), F32),
        compiler_params=pltpu.CompilerParams(
            dimension_semantics=("arbitrary", "arbitrary"), vmem_limit_bytes=VMEM_LIMIT),
        name="ffn",
    )(thr, x, x, x, gain, w_up, w_up, conv_w, conv_w, conv_b, conv_b, w_down, final_gain)


def _pack_sequences(h, x, meta_tokens, first_blk, nb_seq):
    batch = x.shape[0]
    n_blocks = h.shape[0] // BLK
    meta = meta_tokens.astype(h.dtype)
    if batch > 1 and first_blk % nb_seq == 0 and n_blocks % nb_seq == 0:
        slots = h.reshape(n_blocks // nb_seq, nb_seq, BLK, D_MODEL)
        first = first_blk // nb_seq
        slots = slots.at[first:first + batch, 1:].set(x.reshape(batch, nb_seq - 1, BLK, D_MODEL))
        slots = slots.at[first:first + batch, 0, PAD_ROWS:].set(
            jnp.broadcast_to(meta[None], (batch, N_META, D_MODEL)))
        return slots.reshape(h.shape)
    for b in range(batch):
        r0 = (first_blk + b * nb_seq) * BLK
        h = h.at[r0 + PAD_ROWS:r0 + BLK].set(meta)
        h = h.at[r0 + BLK:r0 + nb_seq * BLK].set(x[b])
    return h


def _unpack_sequences(h, first_blk, batch, nb_seq):
    n_blocks = h.shape[0] // BLK
    if batch == 1:
        return h[(first_blk + 1) * BLK:(first_blk + nb_seq) * BLK][None]
    if first_blk % nb_seq == 0 and n_blocks % nb_seq == 0:
        slots = h.reshape(n_blocks // nb_seq, nb_seq, BLK, D_MODEL)
        first = first_blk // nb_seq
        return slots[first:first + batch, 1:].reshape(batch, (nb_seq - 1) * BLK, D_MODEL)
    rows = h[first_blk * BLK:(first_blk + batch * nb_seq) * BLK]
    return rows.reshape(batch, nb_seq * BLK, D_MODEL)[:, BLK:]


def _in_weights(w):
    q = w[:, :ATTN_WIDTH] * (HEAD_DIM ** -0.5 * LOG2_E)
    k = w[:, ATTN_WIDTH:ATTN_WIDTH + KV_WIDTH].reshape(D_MODEL, ATTN_KV_HEADS, 1, HEAD_DIM)
    v = w[:, ATTN_WIDTH + KV_WIDTH:ATTN_WIDTH + 2 * KV_WIDTH].reshape(D_MODEL, ATTN_KV_HEADS, 1, HEAD_DIM)
    kk = jnp.broadcast_to(k, (D_MODEL, ATTN_KV_HEADS, 2, HEAD_DIM)).reshape(D_MODEL, KK_WIDTH)
    vv = jnp.broadcast_to(v, (D_MODEL, ATTN_KV_HEADS, 2, HEAD_DIM)).reshape(D_MODEL, KK_WIDTH)
    hg = w[:, ATTN_WIDTH + 2 * KV_WIDTH:]
    return jnp.concatenate([q, kk, vv, hg], axis=1).astype(BF16)


def _lower_bound_params(lb_param, layer):
    c = jnp.cumsum(jax.nn.softmax(lb_param.astype(F32), axis=1), axis=1)
    lb = (c - c[:, :1])[:, layer]
    rows = [jnp.log(lb[0]), jnp.log1p(-lb[0]), jnp.log(lb[1]), jnp.log1p(-lb[1])]
    rows += [jnp.zeros_like(lb[0])] * 4
    return jnp.stack(rows, axis=0)


def kernel(x_prompt, x_sample, meta_tokens, mix_norm, w_in, attn_sink, attn_out_norm, hgrn_lower_bounds,
           hgrn_out_norm, w_out, ffn_norm, w_up, conv_w, conv_b, w_down, final_norm):
    bp, sp, _ = x_prompt.shape
    bs, ss, _ = x_sample.shape
    assert sp % BLK == 0 and ss % BLK == 0 and x_prompt.shape[2] == x_sample.shape[2] == D_MODEL
    depth = w_in.shape[0]
    lay = _make_layout(bp, sp, bs, ss)
    tables = _block_tables(lay)
    thr = jnp.asarray(tables[0])
    bias = _attn_bias_tables()
    consts = _hgrn_consts()

    rows = lay.n_blocks * BLK
    h = jnp.zeros((rows, D_MODEL), F32)
    h = _pack_sequences(h, x_sample, meta_tokens, 0, lay.nb_s)
    h = _pack_sequences(h, x_prompt, meta_tokens, lay.base_p, lay.nb_p)

    row2 = lambda t: t.reshape(1, -1)
    for l in range(depth):
        proj = _inproj(h, row2(mix_norm[l]), _in_weights(w_in[l]))
        attn = _attention(proj, tables, bias, attn_sink[l] * LOG2_E, row2(attn_out_norm[l]))
        lbp = _lower_bound_params(hgrn_lower_bounds, l)
        gn = row2(hgrn_out_norm[l])
        rec = jnp.zeros((rows, HG_WIDTH), BF16)
        rec = _hgrn_group(proj, rec, lbp, gn, consts, base_blk=0, batch=bs, nb_seq=lay.nb_s)
        rec = _hgrn_group(proj, rec, lbp, gn, consts, base_blk=lay.base_p, batch=bp, nb_seq=lay.nb_p)
        wo = w_out[l].astype(BF16)
        h = _outproj(thr, attn, rec, h, wo[:ATTN_WIDTH], wo[ATTN_WIDTH:])
        h = _ffn(thr, h, row2(ffn_norm[l]), w_up[l].astype(BF16), conv_w[l],
                 row2(conv_b[l]), w_down[l].astype(BF16), row2(final_norm), final=(l == depth - 1))

    return (_unpack_sequences(h, lay.base_p, bp, lay.nb_p), _unpack_sequences(h, 0, bs, lay.nb_s))
```
